```python
import math
import jax, jax.numpy as jnp
from jax import lax
import numpy as np

D_MODEL = 1024
BATCH = 32
SEQ = 2048
DEPTH = 1
DEC_BATCH = 2
DEC_SEQ = 16384
PAST_LEN = 128

N_HEADS = 8
QK_NOPE = 128
QK_ROPE = 64
V_HEAD = 128
Q_LORA = 384
KV_LORA = 256
ROPE_THETA = 10000.0
Q_BLOCK = 128
LRU_WIDTH = 1024
LRU_BLOCKS = 8
LRU_BLOCK = LRU_WIDTH // LRU_BLOCKS
CONV_WIDTH = 4
CONV_PAD_LEFT = 2
CONV_PAD_RIGHT = CONV_WIDTH - 1 - CONV_PAD_LEFT
LRU_C = 8.0
N_DIR = 2
N_GROUPS = 4
EXPERTS_PER_GROUP = 8
N_EXPERTS = N_GROUPS * EXPERTS_PER_GROUP
TOP_K = 2
D_EXPERT = 256
EPS = 1e-6
IN_SPLITS = (Q_LORA, KV_LORA, QK_ROPE, LRU_WIDTH, LRU_WIDTH, D_MODEL, D_MODEL)
IN_COLS = sum(IN_SPLITS)

kernel_name = "hybrid_mla_rglru_hmoe_encoder"


def rmsnorm(x, g):
    xf = x.astype(jnp.float32)
    var = jnp.mean(xf * xf, axis=-1, keepdims=True)
    return (xf * lax.rsqrt(var + EPS) * g.astype(jnp.float32)).astype(x.dtype)


def rope_tables(seq_len):
    inv = ROPE_THETA ** (-jnp.arange(0, QK_ROPE, 2, dtype=jnp.float32) / QK_ROPE)
    ang = jnp.arange(seq_len, dtype=jnp.float32)[:, None] * inv[None, :]
    return jnp.cos(ang), jnp.sin(ang)


def apply_rope(x, cos, sin):
    x1, x2 = jnp.split(x.astype(jnp.float32), 2, axis=-1)
    return jnp.concatenate([x1 * cos - x2 * sin, x2 * cos + x1 * sin], axis=-1).astype(x.dtype)


def mla_branch(q_lat, kv_lat, k_rope, q_a_norm, w_q_b, kv_a_norm, w_kv_b, w_o_attn):
    B, S, _ = q_lat.shape
    q = (rmsnorm(q_lat, q_a_norm) @ w_q_b).reshape(B, S, N_HEADS, QK_NOPE + QK_ROPE)
    q_nope, q_pe = q[..., :QK_NOPE], q[..., QK_NOPE:]
    kv = (rmsnorm(kv_lat, kv_a_norm) @ w_kv_b).reshape(B, S, N_HEADS, QK_NOPE + V_HEAD)
    k_nope, v = kv[..., :QK_NOPE], kv[..., QK_NOPE:]
    cos, sin = rope_tables(S)
    q_pe = apply_rope(q_pe, cos[:, None, :], sin[:, None, :])
    k_pe = apply_rope(k_rope, cos, sin)
    scale = 1.0 / math.sqrt(QK_NOPE + QK_ROPE)
    nb = S // Q_BLOCK
    qn_b = q_nope.reshape(B, nb, Q_BLOCK, N_HEADS, QK_NOPE).transpose(1, 0, 2, 3, 4)
    qp_b = q_pe.reshape(B, nb, Q_BLOCK, N_HEADS, QK_ROPE).transpose(1, 0, 2, 3, 4)

    def block_attn(blk):
        qn, qp = blk
        s = (jnp.einsum('bqhd,bkhd->bhqk', qn, k_nope)
             + jnp.einsum('bqhr,bkr->bhqk', qp, k_pe)).astype(jnp.float32) * scale
        p = jax.nn.softmax(s, axis=-1).astype(v.dtype)
        return jnp.einsum('bhqk,bkhd->bqhd', p, v)

    o = lax.map(block_attn, (qn_b, qp_b))
    o = o.transpose(1, 0, 2, 3, 4).reshape(B, S, N_HEADS * V_HEAD)
    return o @ w_o_attn


def _lin_combine(c1, c2):
    a1, b1 = c1
    a2, b2 = c2
    return (a1 * a2, a2 * b1 + b2)


def rglru_direction(xb, a_w, a_b, x_w, x_b, lam, reverse):
    B, S, W = xb.shape
    xr = xb.reshape(B, S, LRU_BLOCKS, LRU_BLOCK)
    r = jax.nn.sigmoid(jnp.einsum('bsnk,nkj->bsnj', xr, a_w).reshape(B, S, W) + a_b)
    i = jax.nn.sigmoid(jnp.einsum('bsnk,nkj->bsnj', xr, x_w).reshape(B, S, W) + x_b)
    log_a = -LRU_C * r.astype(jnp.float32) * jax.nn.softplus(-lam.astype(jnp.float32))
    a = jnp.exp(log_a)
    bterm = jnp.sqrt(-jnp.expm1(2.0 * log_a)) * (i * xb).astype(jnp.float32)
    _, h = lax.associative_scan(_lin_combine, (a, bterm), axis=1, reverse=reverse)
    return h


def rglru_branch(x_lru, g_lru, conv_w, conv_b, lru_a_w, lru_a_b, lru_x_w, lru_x_b, lru_lambda, w_o_lru):
    B, S, W = x_lru.shape
    xp = jnp.pad(x_lru, ((0, 0), (CONV_PAD_LEFT, CONV_PAD_RIGHT), (0, 0)))
    xc = conv_b + sum(xp[:, k:k + S, :] * conv_w[k] for k in range(CONV_WIDTH))
    h_f = rglru_direction(xc, lru_a_w[0], lru_a_b[0], lru_x_w[0], lru_x_b[0], lru_lambda[0], False)
    h_b = rglru_direction(xc, lru_a_w[1], lru_a_b[1], lru_x_w[1], lru_x_b[1], lru_lambda[1], True)
    y = (h_f + h_b).astype(x_lru.dtype) * jax.nn.gelu(g_lru)
    return y @ w_o_lru


def mixer_block(xn, w_in, q_a_norm, w_q_b, kv_a_norm, w_kv_b, w_o_attn, conv_w, conv_b,
                lru_a_w, lru_a_b, lru_x_w, lru_x_b, lru_lambda, w_o_lru, gate_b, w_out):
    proj = xn @ w_in
    idx = list(np.cumsum(IN_SPLITS)[:-1])
    q_lat, kv_lat, k_rope, x_lru, g_lru, ga_pre, gb_pre = jnp.split(proj, idx, axis=-1)
    y_attn = mla_branch(q_lat, kv_lat, k_rope, q_a_norm, w_q_b, kv_a_norm, w_kv_b, w_o_attn)
    y_lru = rglru_branch(x_lru, g_lru, conv_w, conv_b, lru_a_w, lru_a_b, lru_x_w, lru_x_b,
                         lru_lambda, w_o_lru)
    g_attn = jax.nn.sigmoid(ga_pre + gate_b[0])
    g_rec = jax.nn.sigmoid(gb_pre + gate_b[1])
    return (g_attn * y_attn + g_rec * y_lru) @ w_out


def hmoe_block(xn, router_group_w, router_group_b, router_expert_w, router_expert_b,
               w_gate_e, w_up_e, w_down_e):
    B, S, D = xn.shape
    xf = xn.reshape(B * S, D)
    T = xf.shape[0]
    glog = (xf @ router_group_w).astype(jnp.float32) + router_group_b.astype(jnp.float32)
    gprob = jax.nn.softmax(glog, axis=-1)
    g_idx = jnp.argmax(glog, axis=-1)
    g_p = jnp.take_along_axis(gprob, g_idx[:, None], axis=-1)
    elog = ((xf @ router_expert_w).astype(jnp.float32)
            + router_expert_b.astype(jnp.float32)).reshape(T, N_GROUPS, EXPERTS_PER_GROUP)
    esel = jnp.take_along_axis(elog, g_idx[:, None, None], axis=1)[:, 0]
    eprob = jax.nn.softmax(esel, axis=-1)
    topv, topi = lax.top_k(eprob, TOP_K)
    w = g_p * topv / jnp.sum(topv, axis=-1, keepdims=True)
    ids = g_idx[:, None] * EXPERTS_PER_GROUP + topi
    combine = jnp.sum(jax.nn.one_hot(ids, N_EXPERTS, dtype=jnp.float32) * w[..., None], axis=1)
    combine = combine.astype(xf.dtype)
    out = jnp.zeros_like(xf)
    for e in range(N_EXPERTS):
        h = jax.nn.silu(xf @ w_gate_e[e]) * (xf @ w_up_e[e])
        out = out + combine[:, e:e + 1] * (h @ w_down_e[e])
    return out.reshape(B, S, D)


def setup_inputs(seed: int = 0) -> dict:
    key = jax.random.key(seed)
    ks = jax.random.split(key, 32)
    f32 = jnp.float32

    def nrm(k, shape, fan_in):
        return jax.random.normal(k, shape, f32) * (fan_in ** -0.5)

    def gain(k, shape):
        return 1.0 + 0.02 * jax.random.normal(k, shape, f32)

    def small(k, shape, s=0.02):
        return s * jax.random.normal(k, shape, f32)

    a0 = jax.random.uniform(ks[16], (DEPTH, N_DIR, LRU_WIDTH), f32, 0.9, 0.999)
    a_base = a0 ** (1.0 / LRU_C)
    lru_lambda = jnp.log(a_base) - jnp.log1p(-a_base)
    return {
        "x_prompt": jax.random.normal(ks[0], (BATCH, SEQ, D_MODEL), f32),
        "x_sample": jax.random.normal(ks[1], (DEC_BATCH, DEC_SEQ, D_MODEL), f32),
        "mix_norm": gain(ks[2], (DEPTH, D_MODEL)),
        "w_in": nrm(ks[3], (DEPTH, D_MODEL, IN_COLS), D_MODEL),
        "q_a_norm": gain(ks[4], (DEPTH, Q_LORA)),
        "w_q_b": nrm(ks[5], (DEPTH, Q_LORA, N_HEADS * (QK_NOPE + QK_ROPE)), Q_LORA),
        "kv_a_norm": gain(ks[6], (DEPTH, KV_LORA)),
        "w_kv_b": nrm(ks[7], (DEPTH, KV_LORA, N_HEADS * (QK_NOPE + V_HEAD)), KV_LORA),
        "w_o_attn": nrm(ks[8], (DEPTH, N_HEADS * V_HEAD, D_MODEL), N_HEADS * V_HEAD),
        "conv_w": nrm(ks[9], (DEPTH, CONV_WIDTH, LRU_WIDTH), CONV_WIDTH),
        "conv_b": small(ks[10], (DEPTH, LRU_WIDTH)),
        "lru_a_w": nrm(ks[11], (DEPTH, N_DIR, LRU_BLOCKS, LRU_BLOCK, LRU_BLOCK), LRU_BLOCK),
        "lru_a_b": small(ks[12], (DEPTH, N_DIR, LRU_WIDTH)),
        "lru_x_w": nrm(ks[13], (DEPTH, N_DIR, LRU_BLOCKS, LRU_BLOCK, LRU_BLOCK), LRU_BLOCK),
        "lru_x_b": small(ks[14], (DEPTH, N_DIR, LRU_WIDTH)),
        "lru_lambda": lru_lambda,
        "w_o_lru": nrm(ks[15], (DEPTH, LRU_WIDTH, D_MODEL), LRU_WIDTH),
        "gate_b": small(ks[17], (DEPTH, 2, D_MODEL)),
        "w_out": nrm(ks[18], (DEPTH, D_MODEL, D_MODEL), D_MODEL),
        "ffn_norm": gain(ks[19], (DEPTH, D_MODEL)),
        "router_group_w": nrm(ks[20], (DEPTH, D_MODEL, N_GROUPS), D_MODEL),
        "router_group_b": small(ks[21], (DEPTH, N_GROUPS), 0.01),
        "router_expert_w": nrm(ks[22], (DEPTH, D_MODEL, N_EXPERTS), D_MODEL),
        "router_expert_b": small(ks[23], (DEPTH, N_EXPERTS), 0.01),
        "w_gate_e": nrm(ks[24], (DEPTH, N_EXPERTS, D_MODEL, D_EXPERT), D_MODEL),
        "w_up_e": nrm(ks[25], (DEPTH, N_EXPERTS, D_MODEL, D_EXPERT), D_MODEL),
        "w_down_e": nrm(ks[26], (DEPTH, N_EXPERTS, D_EXPERT, D_MODEL), D_EXPERT),
        "final_norm": gain(ks[27], (D_MODEL,)),
    }


def reference(x_prompt, x_sample, mix_norm, w_in, q_a_norm, w_q_b, kv_a_norm, w_kv_b, w_o_attn,
              conv_w, conv_b, lru_a_w, lru_a_b, lru_x_w, lru_x_b, lru_lambda, w_o_lru, gate_b,
              w_out, ffn_norm, router_group_w, router_group_b, router_expert_w, router_expert_b,
              w_gate_e, w_up_e, w_down_e, final_norm):
    def trunk(x):
        for l in range(DEPTH):
            x = x + mixer_block(rmsnorm(x, mix_norm[l]), w_in[l], q_a_norm[l], w_q_b[l],
                                kv_a_norm[l], w_kv_b[l], w_o_attn[l], conv_w[l], conv_b[l],
                                lru_a_w[l], lru_a_b[l], lru_x_w[l], lru_x_b[l], lru_lambda[l],
                                w_o_lru[l], gate_b[l], w_out[l])
            x = x + hmoe_block(rmsnorm(x, ffn_norm[l]), router_group_w[l], router_group_b[l],
                               router_expert_w[l], router_expert_b[l], w_gate_e[l], w_up_e[l],
                               w_down_e[l])
        return rmsnorm(x, final_norm)

    y_prompt = trunk(x_prompt)
    y_sample = trunk(x_sample)
    return (y_prompt, y_sample)
```

```python
import functools
import math

import jax
import jax.numpy as jnp
from jax import lax
from jax.experimental import pallas as pl
from jax.experimental.pallas import tpu as pltpu

N_HEADS = 8
QK_NOPE = 128
QK_ROPE = 64
V_HEAD = 128
Q_LORA = 384
KV_LORA = 256
ROPE_THETA = 10000.0
LRU_BLOCKS = 8
LRU_C = 8.0
N_GROUPS = 4
EXPERTS_PER_GROUP = 8
N_EXPERTS = N_GROUPS * EXPERTS_PER_GROUP
EPS = 1e-6

LANES = 128
HEAD_PAD = 256
LAT_COLS = Q_LORA + KV_LORA + LANES
VMEM_LIMIT = 56 * 1024 * 1024

F32 = jnp.float32
BF16 = jnp.bfloat16


def _sigmoid(z):
    return 1.0 / (1.0 + jnp.exp(-z))


def _rms(x, g):
    var = jnp.mean(x * x, axis=-1, keepdims=True)
    return x * lax.rsqrt(var + EPS) * g


def _rope(pe, cos_t, nsin_lo, sin_hi):
    x2_to_lo = pltpu.roll(pe, 96, axis=1)
    x1_to_hi = pltpu.roll(pe, 32, axis=1)
    return pe * cos_t + x2_to_lo * nsin_lo + x1_to_hi * sin_hi


def _proj_kernel(x_ref, cos_ref, nsl_ref, sh_ref, mixg_ref, win_ref, qg_ref, wqb_ref, kvg_ref,
                 wkvb_ref, gateb_ref, q_ref, k_ref, v_ref, xlru_ref, gelu_ref, gatt_ref,
                 grec_ref, *, scale):
    d = x_ref.shape[-1]
    xn = _rms(x_ref[...], mixg_ref[...]).astype(BF16)
    lat = jnp.dot(xn, win_ref[:, 0:LAT_COLS], preferred_element_type=F32)
    cos_t, nsl, sh = cos_ref[...], nsl_ref[...], sh_ref[...]

    qn = _rms(lat[:, 0:Q_LORA], qg_ref[...]).astype(BF16)
    qf = jnp.dot(qn, wqb_ref[...], preferred_element_type=F32)
    for h in range(N_HEADS):
        base = h * HEAD_PAD
        q_ref[0, h, :, 0:LANES] = (qf[:, base:base + LANES] * scale).astype(BF16)
        pe = _rope(qf[:, base + LANES:base + HEAD_PAD], cos_t, nsl, sh)
        q_ref[0, h, :, LANES:HEAD_PAD] = (pe * scale).astype(BF16)

    kvn = _rms(lat[:, Q_LORA:Q_LORA + KV_LORA], kvg_ref[...]).astype(BF16)
    kvf = jnp.dot(kvn, wkvb_ref[...], preferred_element_type=F32)
    kpe = _rope(lat[:, Q_LORA + KV_LORA:LAT_COLS], cos_t, nsl, sh).astype(BF16)
    for h in range(N_HEADS):
        k_ref[0, h, :, 0:LANES] = kvf[:, h * LANES:(h + 1) * LANES].astype(BF16)
        k_ref[0, h, :, LANES:HEAD_PAD] = kpe
        v_ref[0, h, :, :] = kvf[:, (N_HEADS + h) * LANES:(N_HEADS + h + 1) * LANES].astype(BF16)

    c0 = LAT_COLS
    xlru_ref[...] = jnp.dot(xn, win_ref[:, c0:c0 + d], preferred_element_type=F32)
    g = jnp.dot(xn, win_ref[:, c0 + d:c0 + 2 * d], preferred_element_type=F32)
    gelu_ref[...] = jax.nn.gelu(g).astype(BF16)
    ga = jnp.dot(xn, win_ref[:, c0 + 2 * d:c0 + 3 * d], preferred_element_type=F32)
    gatt_ref[...] = _sigmoid(ga + gateb_ref[0:1, :]).astype(BF16)
    gb = jnp.dot(xn, win_ref[:, c0 + 3 * d:c0 + 4 * d], preferred_element_type=F32)
    grec_ref[...] = _sigmoid(gb + gateb_ref[1:2, :]).astype(BF16)


def _proj_call(x2d, B, S, tabs, mixg, win, qg, wqb, kvg, wkvb, gateb, tm):
    T, d = x2d.shape
    nt_seq = S // tm
    full = lambda a: pl.BlockSpec(a.shape, lambda i: (0,) * a.ndim)
    tab_spec = pl.BlockSpec((tm, LANES), lambda i: (i % nt_seq, 0))
    tok = lambda w: pl.BlockSpec((tm, w), lambda i: (i, 0))
    head_spec = lambda w: pl.BlockSpec((1, N_HEADS, tm, w), lambda i: (i // nt_seq, 0, i % nt_seq, 0))
    scale = 1.0 / math.sqrt(QK_NOPE + QK_ROPE)
    return pl.pallas_call(
        functools.partial(_proj_kernel, scale=scale),
        grid=(T // tm,),
        in_specs=[tok(d), tab_spec, tab_spec, tab_spec, full(mixg), full(win), full(qg), full(wqb),
                  full(kvg), full(wkvb), full(gateb)],
        out_specs=[head_spec(HEAD_PAD), head_spec(HEAD_PAD), head_spec(V_HEAD),
                   tok(d), tok(d), tok(d), tok(d)],
        out_shape=[
            jax.ShapeDtypeStruct((B, N_HEADS, S, HEAD_PAD), BF16),
            jax.ShapeDtypeStruct((B, N_HEADS, S, HEAD_PAD), BF16),
            jax.ShapeDtypeStruct((B, N_HEADS, S, V_HEAD), BF16),
            jax.ShapeDtypeStruct((T, d), F32),
            jax.ShapeDtypeStruct((T, d), BF16),
            jax.ShapeDtypeStruct((T, d), BF16),
            jax.ShapeDtypeStruct((T, d), BF16),
        ],
        compiler_params=pltpu.CompilerParams(dimension_semantics=("arbitrary",),
                                             vmem_limit_bytes=VMEM_LIMIT),
        name="proj",
    )(x2d, *tabs, mixg, win, qg, wqb, kvg, wkvb, gateb)


def _attn_kernel(q_ref, k_ref, v_ref, o_ref, m_ref, l_ref, acc_ref, *, tk):
    q = q_ref[0, 0]
    nk = k_ref.shape[2] // tk
    m_ref[...] = jnp.full(m_ref.shape, -jnp.inf, F32)
    l_ref[...] = jnp.zeros(l_ref.shape, F32)
    acc_ref[...] = jnp.zeros(acc_ref.shape, F32)

    def body(j, _):
        off = pl.multiple_of(j * tk, tk)
        kc = k_ref[0, 0, pl.ds(off, tk), :]
        vc = v_ref[0, 0, pl.ds(off, tk), :]
        s = lax.dot_general(q, kc, (((1,), (1,)), ((), ())), preferred_element_type=F32)
        m_old = m_ref[...]
        m_new = jnp.maximum(m_old, jnp.max(s, axis=-1, keepdims=True))
        alpha = jnp.exp(m_old - m_new)
        p = jnp.exp(s - m_new)
        l_ref[...] = alpha * l_ref[...] + jnp.sum(p, axis=-1, keepdims=True)
        acc_ref[...] = alpha * acc_ref[...] + jnp.dot(p.astype(BF16), vc,
                                                      preferred_element_type=F32)
        m_ref[...] = m_new
        return 0

    lax.fori_loop(0, nk, body, 0)
    o_ref[0] = (acc_ref[...] / l_ref[...]).astype(BF16)


def _attn_call(q, k, v, tq, tk):
    B, H, S, _ = q.shape
    return pl.pallas_call(
        functools.partial(_attn_kernel, tk=tk),
        grid=(B, H, S // tq),
        in_specs=[
            pl.BlockSpec((1, 1, tq, HEAD_PAD), lambda b, h, i: (b, h, i, 0)),
            pl.BlockSpec((1, 1, S, HEAD_PAD), lambda b, h, i: (b, h, 0, 0)),
            pl.BlockSpec((1, 1, S, V_HEAD), lambda b, h, i: (b, h, 0, 0)),
        ],
        out_specs=pl.BlockSpec((1, tq, V_HEAD), lambda b, h, i: (b, i, h)),
        out_shape=jax.ShapeDtypeStruct((B, S, H * V_HEAD), BF16),
        scratch_shapes=[pltpu.VMEM((tq, 1), F32), pltpu.VMEM((tq, 1), F32),
                        pltpu.VMEM((tq, V_HEAD), F32)],
        compiler_params=pltpu.CompilerParams(
            dimension_semantics=("arbitrary", "arbitrary", "arbitrary"),
            vmem_limit_bytes=VMEM_LIMIT),
        name="attn",
    )(q, k, v)


def _lru_kernel(*refs, reverse, n_chunks):
    if reverse:
        (x_ref, xp_ref, xn_ref, cw_ref, cb_ref, w_ref, ab_ref, xb_ref, lam_ref,
         out_ref, h_ref, a_s, b_s, h_s) = refs
    else:
        (x_ref, xp_ref, xn_ref, cw_ref, cb_ref, w_ref, ab_ref, xb_ref, lam_ref, hb_ref, gelu_ref,
         out_ref, h_ref, a_s, b_s, h_s) = refs
    j = pl.program_id(1)
    c = (n_chunks - 1 - j) if reverse else j
    L, W = x_ref.shape[1], x_ref.shape[2]
    blk = W // LRU_BLOCKS

    @pl.when(j == 0)
    def _():
        h_ref[...] = jnp.zeros(h_ref.shape, F32)

    x = x_ref[0]
    prev = jnp.where(c > 0, xp_ref[0], 0.0)
    nxt = jnp.where(c < n_chunks - 1, xn_ref[0], 0.0)
    row = lax.broadcasted_iota(jnp.int32, (L, W), 0)
    xm1 = jnp.where(row == 0, prev[7:8, :], pltpu.roll(x, 1, axis=0))
    xm2 = jnp.where(row == 0, prev[6:7, :], jnp.where(row == 1, prev[7:8, :],
                                                       pltpu.roll(x, 2, axis=0)))
    xp1 = jnp.where(row == L - 1, nxt[0:1, :], pltpu.roll(x, L - 1, axis=0))
    xc = cb_ref[...] + (xm2 * cw_ref[0:1, :] + xm1 * cw_ref[1:2, :] + x * cw_ref[2:3, :]
                        + xp1 * cw_ref[3:4, :])
    xcb = xc.astype(BF16)
    sp = jnp.log1p(jnp.exp(-jnp.abs(lam_ref[...]))) + jnp.maximum(-lam_ref[...], 0.0)
    for n in range(LRU_BLOCKS):
        sl = slice(n * blk, (n + 1) * blk)
        z = jnp.dot(xcb[:, sl], w_ref[n], preferred_element_type=F32)
        r = _sigmoid(z[:, 0:blk] + ab_ref[:, sl])
        i = _sigmoid(z[:, blk:2 * blk] + xb_ref[:, sl])
        log_a = (-LRU_C) * r * sp[:, sl]
        a = jnp.exp(log_a)
        a_s[:, sl] = a
        b_s[:, sl] = jnp.sqrt(1.0 - a * a) * (i * xc[:, sl])

    def step(t, h):
        tt = (L - 1 - t) if reverse else t
        h = a_s[pl.ds(tt, 1), :] * h + b_s[pl.ds(tt, 1), :]
        h_s[pl.ds(tt, 1), :] = h
        return h

    h_ref[...] = lax.fori_loop(0, L, step, h_ref[...], unroll=8)
    if reverse:
        out_ref[0] = h_s[...].astype(BF16)
    else:
        hsum = h_s[...] + hb_ref[0].astype(F32)
        out_ref[0] = (hsum * gelu_ref[0].astype(F32)).astype(BF16)


def _lru_call(x_lru, cw, cb, w, ab, xb, lam, hb, gelu, L, reverse):
    B, S, W = x_lru.shape
    nC = S // L
    pos = (lambda j: nC - 1 - j) if reverse else (lambda j: j)
    r8 = L // 8
    full = lambda a: pl.BlockSpec(a.shape, lambda b, j: (0,) * a.ndim)
    chunk = pl.BlockSpec((1, L, W), lambda b, j: (b, pos(j), 0))
    in_specs = [
        chunk,
        pl.BlockSpec((1, 8, W), lambda b, j: (b, jnp.maximum(pos(j) * r8 - 1, 0), 0)),
        pl.BlockSpec((1, 8, W), lambda b, j: (b, jnp.minimum((pos(j) + 1) * r8, S // 8 - 1), 0)),
        full(cw), full(cb), full(w), full(ab), full(xb), full(lam),
    ]
    args = [x_lru, x_lru, x_lru, cw, cb, w, ab, xb, lam]
    if not reverse:
        in_specs += [chunk, chunk]
        args += [hb, gelu]
    return pl.pallas_call(
        functools.partial(_lru_kernel, reverse=reverse, n_chunks=nC),
        grid=(B, nC),
        in_specs=in_specs,
        out_specs=chunk,
        out_shape=jax.ShapeDtypeStruct((B, S, W), BF16),
        scratch_shapes=[pltpu.VMEM((1, W), F32), pltpu.VMEM((L, W), F32), pltpu.VMEM((L, W), F32),
                        pltpu.VMEM((L, W), F32)],
        compiler_params=pltpu.CompilerParams(dimension_semantics=("arbitrary", "arbitrary"),
                                             vmem_limit_bytes=VMEM_LIMIT),
        name="lru_bwd" if reverse else "lru_fwd",
    )(*args)


def _merge_kernel(x_ref, o_ref, yl_ref, gatt_ref, grec_ref, woa_ref, wol_ref, wout_ref, fg_ref,
                  wr_ref, br_ref, x1_ref, xn2_ref, comb_ref):
    y_attn = jnp.dot(o_ref[...], woa_ref[...], preferred_element_type=F32)
    y_lru = jnp.dot(yl_ref[...], wol_ref[...], preferred_element_type=F32)
    mix = gatt_ref[...].astype(F32) * y_attn + grec_ref[...].astype(F32) * y_lru
    x1 = x_ref[...] + jnp.dot(mix.astype(BF16), wout_ref[...], preferred_element_type=F32)
    x1_ref[...] = x1
    xn2 = _rms(x1, fg_ref[...])
    xn2_ref[...] = xn2.astype(BF16)

    logits = jnp.dot(xn2, wr_ref[...], preferred_element_type=F32,
                     precision=lax.Precision.HIGHEST) + br_ref[...]
    lane = lax.broadcasted_iota(jnp.int32, logits.shape, 1)
    big = jnp.int32(LANES)
    neg = jnp.float32(-jnp.inf)
    is_g = (lane >= N_EXPERTS) & (lane < N_EXPERTS + N_GROUPS)
    gl = jnp.where(is_g, logits, neg)
    gmax = jnp.max(gl, axis=-1, keepdims=True)
    g_p = 1.0 / jnp.sum(jnp.exp(gl - gmax), axis=-1, keepdims=True)
    g_idx = jnp.min(jnp.where(gl == gmax, lane - N_EXPERTS, big), axis=-1, keepdims=True)
    in_grp = (lane < N_EXPERTS) & ((lane // EXPERTS_PER_GROUP) == g_idx)
    el = jnp.where(in_grp, logits, neg)
    m1 = jnp.max(el, axis=-1, keepdims=True)
    i1 = jnp.min(jnp.where(el == m1, lane, big), axis=-1, keepdims=True)
    el2 = jnp.where(lane == i1, neg, el)
    m2 = jnp.max(el2, axis=-1, keepdims=True)
    i2 = jnp.min(jnp.where(el2 == m2, lane, big), axis=-1, keepdims=True)
    e2 = jnp.exp(m2 - m1)
    w1 = g_p / (1.0 + e2)
    w2 = w1 * e2
    comb_ref[...] = jnp.where(lane == i1, w1, 0.0) + jnp.where(lane == i2, w2, 0.0)


def _merge_call(x2d, o, yl, gatt, grec, woa, wol, wout, fg, wr, br, tm):
    T, d = x2d.shape
    full = lambda a: pl.BlockSpec(a.shape, lambda i: (0,) * a.ndim)
    tok = lambda w: pl.BlockSpec((tm, w), lambda i: (i, 0))
    return pl.pallas_call(
        _merge_kernel,
        grid=(T // tm,),
        in_specs=[tok(d), tok(d), tok(d), tok(d), tok(d), full(woa), full(wol), full(wout),
                  full(fg), full(wr), full(br)],
        out_specs=[tok(d), tok(d), tok(LANES)],
        out_shape=[jax.ShapeDtypeStruct((T, d), F32), jax.ShapeDtypeStruct((T, d), BF16),
                   jax.ShapeDtypeStruct((T, LANES), F32)],
        compiler_params=pltpu.CompilerParams(dimension_semantics=("arbitrary",),
                                             vmem_limit_bytes=VMEM_LIMIT),
        name="merge",
    )(x2d, o, yl, gatt, grec, woa, wol, wout, fg, wr, br)


def _moe_kernel(xn_ref, x1_ref, comb_ref, wg_ref, wu_ref, wd_ref, fin_ref, out_ref, acc_ref):
    e = pl.program_id(1)

    @pl.when(e == 0)
    def _():
        acc_ref[...] = jnp.zeros(acc_ref.shape, F32)

    xn = xn_ref[...]
    comb = comb_ref[...]
    lane = lax.broadcasted_iota(jnp.int32, comb.shape, 1)
    c_e = jnp.sum(jnp.where(lane == e, comb, 0.0), axis=-1, keepdims=True)
    gt = jnp.dot(xn, wg_ref[0], preferred_element_type=F32)
    up = jnp.dot(xn, wu_ref[0], preferred_element_type=F32)
    h = (gt * _sigmoid(gt)) * up * c_e
    acc_ref[...] += jnp.dot(h.astype(BF16), wd_ref[0], preferred_element_type=F32)

    @pl.when(e == pl.num_programs(1) - 1)
    def _():
        out_ref[...] = _rms(x1_ref[...] + acc_ref[...], fin_ref[...])


def _moe_call(xn2, x1, comb, wg, wu, wd, fin, tm):
    T, d = x1.shape
    E, _, de = wg.shape
    tok = lambda w: pl.BlockSpec((tm, w), lambda i, e: (i, 0))
    return pl.pallas_call(
        _moe_kernel,
        grid=(T // tm, E),
        in_specs=[tok(d), tok(d), tok(LANES),
                  pl.BlockSpec((1, d, de), lambda i, e: (e, 0, 0)),
                  pl.BlockSpec((1, d, de), lambda i, e: (e, 0, 0)),
                  pl.BlockSpec((1, de, d), lambda i, e: (e, 0, 0)),
                  pl.BlockSpec(fin.shape, lambda i, e: (0, 0))],
        out_specs=tok(d),
        out_shape=jax.ShapeDtypeStruct((T, d), F32),
        scratch_shapes=[pltpu.VMEM((tm, d), F32)],
        compiler_params=pltpu.CompilerParams(dimension_semantics=("arbitrary", "arbitrary"),
                                             vmem_limit_bytes=VMEM_LIMIT),
        name="moe",
    )(xn2, x1, comb, wg, wu, wd, fin)


def _rope_tables(S):
    inv = ROPE_THETA ** (-jnp.arange(0, QK_ROPE, 2, dtype=F32) / QK_ROPE)
    ang = jnp.arange(S, dtype=F32)[:, None] * inv[None, :]
    cos, sin = jnp.cos(ang), jnp.sin(ang)
    half = QK_ROPE // 2
    z = lambda n: jnp.zeros((S, n), F32)
    cos_t = jnp.concatenate([cos, cos, z(LANES - QK_ROPE)], axis=1)
    nsin_lo = jnp.concatenate([-sin, z(LANES - half)], axis=1)
    sin_hi = jnp.concatenate([z(half), sin, z(LANES - QK_ROPE)], axis=1)
    return cos_t, nsin_lo, sin_hi


def _prep_params(p):
    d = p["w_in"].shape[0]
    row = lambda a: a.reshape(1, -1).astype(F32)
    w_in = p["w_in"]
    c_q, c_kv, c_r = Q_LORA, Q_LORA + KV_LORA, Q_LORA + KV_LORA + QK_ROPE
    win = jnp.concatenate(
        [w_in[:, :c_r], jnp.zeros((d, LANES - QK_ROPE), F32), w_in[:, c_r:]], axis=1).astype(BF16)
    wqb = p["w_q_b"].reshape(Q_LORA, N_HEADS, QK_NOPE + QK_ROPE)
    wqb = jnp.pad(wqb, ((0, 0), (0, 0), (0, HEAD_PAD - QK_NOPE - QK_ROPE)))
    wqb = wqb.reshape(Q_LORA, N_HEADS * HEAD_PAD).astype(BF16)
    wkvb = p["w_kv_b"].reshape(KV_LORA, N_HEADS, QK_NOPE + V_HEAD)
    wkvb = jnp.concatenate([wkvb[:, :, :QK_NOPE].reshape(KV_LORA, -1),
                            wkvb[:, :, QK_NOPE:].reshape(KV_LORA, -1)], axis=1).astype(BF16)
    wlru = jnp.concatenate([p["lru_a_w"], p["lru_x_w"]], axis=-1).astype(BF16)
    wr = jnp.concatenate([p["router_expert_w"], p["router_group_w"],
                          jnp.zeros((d, LANES - N_EXPERTS - N_GROUPS), F32)], axis=1)
    br = jnp.concatenate([p["router_expert_b"], p["router_group_b"],
                          jnp.zeros((LANES - N_EXPERTS - N_GROUPS,), F32)]).reshape(1, LANES)
    return dict(
        mixg=row(p["mix_norm"]), win=win, qg=row(p["q_a_norm"]), wqb=wqb, kvg=row(p["kv_a_norm"]),
        wkvb=wkvb, gateb=p["gate_b"].astype(F32), cw=p["conv_w"].astype(F32), cb=row(p["conv_b"]),
        wlru=wlru, ab=p["lru_a_b"].astype(F32), xb=p["lru_x_b"].astype(F32),
        lam=p["lru_lambda"].astype(F32), woa=p["w_o_attn"].astype(BF16),
        wol=p["w_o_lru"].astype(BF16), wout=p["w_out"].astype(BF16), fg=row(p["ffn_norm"]),
        wr=wr, br=br, wg=p["w_gate_e"].astype(BF16), wu=p["w_up_e"].astype(BF16),
        wd=p["w_down_e"].astype(BF16), fin=row(p["final_norm"]))


def _trunk(x, w):
    B, S, d = x.shape
    T = B * S
    x2d = x.reshape(T, d)
    tm = min(256, S)
    tabs = _rope_tables(S)
    q, k, v, x_lru, gelu_g, gatt, grec = _proj_call(
        x2d, B, S, tabs, w["mixg"], w["win"], w["qg"], w["wqb"], w["kvg"], w["wkvb"], w["gateb"], tm)
    o = _attn_call(q, k, v, tq=min(512, S), tk=min(512, S))

    L = min(256, S)
    x_lru3 = x_lru.reshape(B, S, d)
    gelu3 = gelu_g.reshape(B, S, d)
    hb = _lru_call(x_lru3, w["cw"], w["cb"], w["wlru"][1], w["ab"][1:2], w["xb"][1:2],
                   w["lam"][1:2], None, None, L, reverse=True)
    yl = _lru_call(x_lru3, w["cw"], w["cb"], w["wlru"][0], w["ab"][0:1], w["xb"][0:1],
                   w["lam"][0:1], hb, gelu3, L, reverse=False)

    x1, xn2, comb = _merge_call(x2d, o.reshape(T, d), yl.reshape(T, d), gatt, grec, w["woa"],
                                w["wol"], w["wout"], w["fg"], w["wr"], w["br"], tm=min(512, S))
    y = _moe_call(xn2, x1, comb, w["wg"], w["wu"], w["wd"], w["fin"], tm=min(1024, S))
    return y.reshape(B, S, d)


def kernel(x_prompt, x_sample, mix_norm, w_in, q_a_norm, w_q_b, kv_a_norm, w_kv_b, w_o_attn, conv_w, conv_b, lru_a_w, lru_a_b, lru_x_w, lru_x_b, lru_lambda, w_o_lru, gate_b, w_out, ffn_norm, router_group_w, router_group_b, router_expert_w, router_expert_b, w_gate_e, w_up_e, w_down_e, final_norm):
    depth = mix_norm.shape[0]
    layers = []
    for l in range(depth):
        layers.append(_prep_params(dict(
            mix_norm=mix_norm[l], w_in=w_in[l], q_a_norm=q_a_norm[l], w_q_b=w_q_b[l],
            kv_a_norm=kv_a_norm[l], w_kv_b=w_kv_b[l], w_o_attn=w_o_attn[l], conv_w=conv_w[l],
            conv_b=conv_b[l], lru_a_w=lru_a_w[l], lru_a_b=lru_a_b[l], lru_x_w=lru_x_w[l],
            lru_x_b=lru_x_b[l], lru_lambda=lru_lambda[l], w_o_lru=w_o_lru[l], gate_b=gate_b[l],
            w_out=w_out[l], ffn_norm=ffn_norm[l], router_group_w=router_group_w[l],
            router_group_b=router_group_b[l], router_expert_w=router_expert_w[l],
            router_expert_b=router_expert_b[l], w_gate_e=w_gate_e[l], w_up_e=w_up_e[l],
            w_down_e=w_down_e[l], final_norm=final_norm)))
    assert depth == 1

    def trunk(x):
        for w in layers:
            x = _trunk(x, w)
        return x

    return (trunk(x_prompt), trunk(x_sample))
```

```python
import functools
import math

import jax
import jax.numpy as jnp
from jax import lax
from jax.experimental import pallas as pl
from jax.experimental.pallas import tpu as pltpu

N_HEADS = 8
QK_NOPE = 128
QK_ROPE = 64
V_HEAD = 128
Q_LORA = 384
KV_LORA = 256
ROPE_THETA = 10000.0
LRU_BLOCKS = 8
LRU_C = 8.0
N_GROUPS = 4
EXPERTS_PER_GROUP = 8
N_EXPERTS = N_GROUPS * EXPERTS_PER_GROUP
EPS = 1e-6

LANES = 128
HEAD_PAD = 256
LAT_COLS = Q_LORA + KV_LORA + LANES
VMEM_LIMIT = 56 * 1024 * 1024

F32 = jnp.float32
BF16 = jnp.bfloat16


def _sigmoid(z):
    return 1.0 / (1.0 + jnp.exp(-z))


def _rms(x, g):
    var = jnp.mean(x * x, axis=-1, keepdims=True)
    return x * lax.rsqrt(var + EPS) * g


def _rope(pe, cos_t, nsin_lo, sin_hi):
    x2_to_lo = pltpu.roll(pe, 96, axis=1)
    x1_to_hi = pltpu.roll(pe, 32, axis=1)
    return pe * cos_t + x2_to_lo * nsin_lo + x1_to_hi * sin_hi


def _proj_kernel(x_ref, cos_ref, nsl_ref, sh_ref, mixg_ref, win_ref, qg_ref, wqb_ref, kvg_ref,
                 wkvb_ref, gateb_ref, q_ref, k_ref, v_ref, xlru_ref, gelu_ref, gatt_ref,
                 grec_ref, *, scale):
    d = x_ref.shape[-1]
    xn = _rms(x_ref[...], mixg_ref[...]).astype(BF16)
    lat = jnp.dot(xn, win_ref[:, 0:LAT_COLS], preferred_element_type=F32)
    cos_t, nsl, sh = cos_ref[...], nsl_ref[...], sh_ref[...]

    qn = _rms(lat[:, 0:Q_LORA], qg_ref[...]).astype(BF16)
    qf = jnp.dot(qn, wqb_ref[...], preferred_element_type=F32)
    for h in range(N_HEADS):
        base = h * HEAD_PAD
        q_ref[0, h, :, 0:LANES] = (qf[:, base:base + LANES] * scale).astype(BF16)
        pe = _rope(qf[:, base + LANES:base + HEAD_PAD], cos_t, nsl, sh)
        q_ref[0, h, :, LANES:HEAD_PAD] = (pe * scale).astype(BF16)

    kvn = _rms(lat[:, Q_LORA:Q_LORA + KV_LORA], kvg_ref[...]).astype(BF16)
    kvf = jnp.dot(kvn, wkvb_ref[...], preferred_element_type=F32)
    kpe = _rope(lat[:, Q_LORA + KV_LORA:LAT_COLS], cos_t, nsl, sh).astype(BF16)
    for h in range(N_HEADS):
        k_ref[0, h, :, 0:LANES] = kvf[:, h * LANES:(h + 1) * LANES].astype(BF16)
        k_ref[0, h, :, LANES:HEAD_PAD] = kpe
        v_ref[0, h, :, :] = kvf[:, (N_HEADS + h) * LANES:(N_HEADS + h + 1) * LANES].astype(BF16)

    c0 = LAT_COLS
    xlru_ref[...] = jnp.dot(xn, win_ref[:, c0:c0 + d], preferred_element_type=F32)
    g = jnp.dot(xn, win_ref[:, c0 + d:c0 + 2 * d], preferred_element_type=F32)
    gelu_ref[...] = jax.nn.gelu(g).astype(BF16)
    ga = jnp.dot(xn, win_ref[:, c0 + 2 * d:c0 + 3 * d], preferred_element_type=F32)
    gatt_ref[...] = _sigmoid(ga + gateb_ref[0:1, :]).astype(BF16)
    gb = jnp.dot(xn, win_ref[:, c0 + 3 * d:c0 + 4 * d], preferred_element_type=F32)
    grec_ref[...] = _sigmoid(gb + gateb_ref[1:2, :]).astype(BF16)


def _proj_call(x2d, B, S, tabs, mixg, win, qg, wqb, kvg, wkvb, gateb, tm):
    T, d = x2d.shape
    nt_seq = S // tm
    full = lambda a: pl.BlockSpec(a.shape, lambda i: (0,) * a.ndim)
    tab_spec = pl.BlockSpec((tm, LANES), lambda i: (i % nt_seq, 0))
    tok = lambda w: pl.BlockSpec((tm, w), lambda i: (i, 0))
    head_spec = lambda w: pl.BlockSpec((1, N_HEADS, tm, w), lambda i: (i // nt_seq, 0, i % nt_seq, 0))
    scale = math.log2(math.e) / math.sqrt(QK_NOPE + QK_ROPE)
    return pl.pallas_call(
        functools.partial(_proj_kernel, scale=scale),
        grid=(T // tm,),
        in_specs=[tok(d), tab_spec, tab_spec, tab_spec, full(mixg), full(win), full(qg), full(wqb),
                  full(kvg), full(wkvb), full(gateb)],
        out_specs=[head_spec(HEAD_PAD), head_spec(HEAD_PAD), head_spec(V_HEAD),
                   tok(d), tok(d), tok(d), tok(d)],
        out_shape=[
            jax.ShapeDtypeStruct((B, N_HEADS, S, HEAD_PAD), BF16),
            jax.ShapeDtypeStruct((B, N_HEADS, S, HEAD_PAD), BF16),
            jax.ShapeDtypeStruct((B, N_HEADS, S, V_HEAD), BF16),
            jax.ShapeDtypeStruct((T, d), F32),
            jax.ShapeDtypeStruct((T, d), BF16),
            jax.ShapeDtypeStruct((T, d), BF16),
            jax.ShapeDtypeStruct((T, d), BF16),
        ],
        compiler_params=pltpu.CompilerParams(dimension_semantics=("arbitrary",),
                                             vmem_limit_bytes=VMEM_LIMIT),
        name="proj",
    )(x2d, *tabs, mixg, win, qg, wqb, kvg, wkvb, gateb)


def _attn_kernel(q_ref, k_ref, v_ref, o_ref, s_ref, m_ref, l_ref, acc_ref, *, tk, unroll):
    q = q_ref[0, 0]
    nk = k_ref.shape[2] // tk
    n_rep = tk // LANES

    def scores(j):
        off = j * tk if isinstance(j, int) else pl.multiple_of(j * tk, tk)
        kc = k_ref[0, 0, pl.ds(off, tk), :]
        return lax.dot_general(q, kc, (((1,), (1,)), ((), ())), preferred_element_type=F32)

    def consume(j, s):
        off = j * tk if isinstance(j, int) else pl.multiple_of(j * tk, tk)
        m_old = m_ref[...]
        m_new = jnp.maximum(m_old, jnp.max(s, axis=-1, keepdims=True))
        alpha = jnp.exp2(m_old - m_new)
        p = jnp.exp2(s - pltpu.repeat(m_new, n_rep, axis=1))
        psum = p[:, 0:LANES]
        for c in range(1, n_rep):
            psum = psum + p[:, c * LANES:(c + 1) * LANES]
        l_ref[...] = alpha * l_ref[...] + psum
        pv = jnp.dot(p.astype(BF16), v_ref[0, 0, pl.ds(off, tk), :], preferred_element_type=F32)
        acc_ref[...] = alpha * acc_ref[...] + pv
        m_ref[...] = m_new

    m_ref[...] = jnp.full(m_ref.shape, -jnp.inf, F32)
    l_ref[...] = jnp.zeros(l_ref.shape, F32)
    acc_ref[...] = jnp.zeros(acc_ref.shape, F32)
    s_ref[0] = scores(0)

    def step(j, cur):
        s = s_ref[cur]
        s_ref[1 - cur] = scores(j + 1)
        consume(j, s)

    if nk <= unroll:
        for j in range(nk - 1):
            step(j, j % 2)
    else:
        assert unroll % 2 == 0 and nk % unroll == 0
        def group(g, _):
            for u in range(unroll):
                step(unroll * g + u, u % 2)
            return 0
        lax.fori_loop(0, nk // unroll - 1, group, 0)
        for j in range(nk - unroll, nk - 1):
            step(j, j % 2)
    consume(nk - 1, s_ref[(nk - 1) % 2])
    l = jnp.sum(l_ref[...], axis=-1, keepdims=True)
    o_ref[0] = (acc_ref[...] / l).astype(BF16)


def _attn_call(q, k, v, tq, tk, unroll=8):
    B, H, S, _ = q.shape
    return pl.pallas_call(
        functools.partial(_attn_kernel, tk=tk, unroll=unroll),
        grid=(B, H, S // tq),
        in_specs=[
            pl.BlockSpec((1, 1, tq, HEAD_PAD), lambda b, h, i: (b, h, i, 0)),
            pl.BlockSpec((1, 1, S, HEAD_PAD), lambda b, h, i: (b, h, 0, 0)),
            pl.BlockSpec((1, 1, S, V_HEAD), lambda b, h, i: (b, h, 0, 0)),
        ],
        out_specs=pl.BlockSpec((1, tq, V_HEAD), lambda b, h, i: (b, i, h)),
        out_shape=jax.ShapeDtypeStruct((B, S, H * V_HEAD), BF16),
        scratch_shapes=[pltpu.VMEM((2, tq, tk), F32), pltpu.VMEM((tq, LANES), F32),
                        pltpu.VMEM((tq, LANES), F32), pltpu.VMEM((tq, V_HEAD), F32)],
        compiler_params=pltpu.CompilerParams(
            dimension_semantics=("arbitrary", "arbitrary", "arbitrary"),
            vmem_limit_bytes=VMEM_LIMIT),
        name="attn",
    )(q, k, v)


def _lru_kernel(*refs, reverse, n_chunks):
    if reverse:
        (x_ref, xp_ref, xn_ref, cw_ref, cb_ref, w_ref, ab_ref, xb_ref, lam_ref,
         out_ref, h_ref, a_s, b_s, h_s) = refs
    else:
        (x_ref, xp_ref, xn_ref, cw_ref, cb_ref, w_ref, ab_ref, xb_ref, lam_ref, hb_ref, gelu_ref,
         out_ref, h_ref, a_s, b_s, h_s) = refs
    j = pl.program_id(1)
    c = (n_chunks - 1 - j) if reverse else j
    L, W = x_ref.shape[1], x_ref.shape[2]
    blk = W // LRU_BLOCKS

    @pl.when(j == 0)
    def _():
        h_ref[...] = jnp.zeros(h_ref.shape, F32)

    x = x_ref[0]
    prev = jnp.where(c > 0, xp_ref[0], 0.0)
    nxt = jnp.where(c < n_chunks - 1, xn_ref[0], 0.0)
    row = lax.broadcasted_iota(jnp.int32, (L, W), 0)
    xm1 = jnp.where(row == 0, prev[7:8, :], pltpu.roll(x, 1, axis=0))
    xm2 = jnp.where(row == 0, prev[6:7, :], jnp.where(row == 1, prev[7:8, :],
                                                       pltpu.roll(x, 2, axis=0)))
    xp1 = jnp.where(row == L - 1, nxt[0:1, :], pltpu.roll(x, L - 1, axis=0))
    xc = cb_ref[...] + (xm2 * cw_ref[0:1, :] + xm1 * cw_ref[1:2, :] + x * cw_ref[2:3, :]
                        + xp1 * cw_ref[3:4, :])
    xcb = xc.astype(BF16)
    sp = jnp.log1p(jnp.exp(-jnp.abs(lam_ref[...]))) + jnp.maximum(-lam_ref[...], 0.0)
    for n in range(LRU_BLOCKS):
        sl = slice(n * blk, (n + 1) * blk)
        z = jnp.dot(xcb[:, sl], w_ref[n], preferred_element_type=F32)
        r = _sigmoid(z[:, 0:blk] + ab_ref[:, sl])
        i = _sigmoid(z[:, blk:2 * blk] + xb_ref[:, sl])
        log_a = (-LRU_C) * r * sp[:, sl]
        a = jnp.exp(log_a)
        a_s[:, sl] = a
        b_s[:, sl] = jnp.sqrt(1.0 - a * a) * (i * xc[:, sl])

    def step(t, h):
        tt = (L - 1 - t) if reverse else t
        h = a_s[pl.ds(tt, 1), :] * h + b_s[pl.ds(tt, 1), :]
        h_s[pl.ds(tt, 1), :] = h
        return h

    h_ref[...] = lax.fori_loop(0, L, step, h_ref[...], unroll=8)
    if reverse:
        out_ref[0] = h_s[...].astype(BF16)
    else:
        hsum = h_s[...] + hb_ref[0].astype(F32)
        out_ref[0] = (hsum * gelu_ref[0].astype(F32)).astype(BF16)


def _lru_call(x_lru, cw, cb, w, ab, xb, lam, hb, gelu, L, reverse):
    B, S, W = x_lru.shape
    nC = S // L
    pos = (lambda j: nC - 1 - j) if reverse else (lambda j: j)
    r8 = L // 8
    full = lambda a: pl.BlockSpec(a.shape, lambda b, j: (0,) * a.ndim)
    chunk = pl.BlockSpec((1, L, W), lambda b, j: (b, pos(j), 0))
    in_specs = [
        chunk,
        pl.BlockSpec((1, 8, W), lambda b, j: (b, jnp.maximum(pos(j) * r8 - 1, 0), 0)),
        pl.BlockSpec((1, 8, W), lambda b, j: (b, jnp.minimum((pos(j) + 1) * r8, S // 8 - 1), 0)),
        full(cw), full(cb), full(w), full(ab), full(xb), full(lam),
    ]
    args = [x_lru, x_lru, x_lru, cw, cb, w, ab, xb, lam]
    if not reverse:
        in_specs += [chunk, chunk]
        args += [hb, gelu]
    return pl.pallas_call(
        functools.partial(_lru_kernel, reverse=reverse, n_chunks=nC),
        grid=(B, nC),
        in_specs=in_specs,
        out_specs=chunk,
        out_shape=jax.ShapeDtypeStruct((B, S, W), BF16),
        scratch_shapes=[pltpu.VMEM((1, W), F32), pltpu.VMEM((L, W), F32), pltpu.VMEM((L, W), F32),
                        pltpu.VMEM((L, W), F32)],
        compiler_params=pltpu.CompilerParams(dimension_semantics=("arbitrary", "arbitrary"),
                                             vmem_limit_bytes=VMEM_LIMIT),
        name="lru_bwd" if reverse else "lru_fwd",
    )(*args)


def _merge_kernel(x_ref, o_ref, yl_ref, gatt_ref, grec_ref, woa_ref, wol_ref, wout_ref, fg_ref,
                  wr_ref, br_ref, x1_ref, xn2_ref, comb_ref):
    y_attn = jnp.dot(o_ref[...], woa_ref[...], preferred_element_type=F32)
    y_lru = jnp.dot(yl_ref[...], wol_ref[...], preferred_element_type=F32)
    mix = gatt_ref[...].astype(F32) * y_attn + grec_ref[...].astype(F32) * y_lru
    x1 = x_ref[...] + jnp.dot(mix.astype(BF16), wout_ref[...], preferred_element_type=F32)
    x1_ref[...] = x1
    xn2 = _rms(x1, fg_ref[...])
    xn2_ref[...] = xn2.astype(BF16)

    x_hi = xn2.astype(BF16)
    x_lo = (xn2 - x_hi.astype(F32)).astype(BF16)
    hh = jnp.dot(x_hi, wr_ref[...], preferred_element_type=F32)
    lh = jnp.dot(x_lo, wr_ref[:, 0:LANES], preferred_element_type=F32)
    logits = hh[:, 0:LANES] + (hh[:, LANES:2 * LANES] + lh) + br_ref[...]
    lane = lax.broadcasted_iota(jnp.int32, logits.shape, 1)
    big = jnp.int32(LANES)
    neg = jnp.float32(-jnp.inf)
    is_g = (lane >= N_EXPERTS) & (lane < N_EXPERTS + N_GROUPS)
    gl = jnp.where(is_g, logits, neg)
    gmax = jnp.max(gl, axis=-1, keepdims=True)
    g_p = 1.0 / jnp.sum(jnp.exp(gl - gmax), axis=-1, keepdims=True)
    g_idx = jnp.min(jnp.where(gl == gmax, lane - N_EXPERTS, big), axis=-1, keepdims=True)
    in_grp = (lane < N_EXPERTS) & ((lane // EXPERTS_PER_GROUP) == g_idx)
    el = jnp.where(in_grp, logits, neg)
    m1 = jnp.max(el, axis=-1, keepdims=True)
    i1 = jnp.min(jnp.where(el == m1, lane, big), axis=-1, keepdims=True)
    el2 = jnp.where(lane == i1, neg, el)
    m2 = jnp.max(el2, axis=-1, keepdims=True)
    i2 = jnp.min(jnp.where(el2 == m2, lane, big), axis=-1, keepdims=True)
    e2 = jnp.exp(m2 - m1)
    w1 = g_p / (1.0 + e2)
    w2 = w1 * e2
    comb_ref[...] = jnp.where(lane == i1, w1, 0.0) + jnp.where(lane == i2, w2, 0.0)


def _merge_call(x2d, o, yl, gatt, grec, woa, wol, wout, fg, wr, br, tm):
    T, d = x2d.shape
    full = lambda a: pl.BlockSpec(a.shape, lambda i: (0,) * a.ndim)
    tok = lambda w: pl.BlockSpec((tm, w), lambda i: (i, 0))
    return pl.pallas_call(
        _merge_kernel,
        grid=(T // tm,),
        in_specs=[tok(d), tok(d), tok(d), tok(d), tok(d), full(woa), full(wol), full(wout),
                  full(fg), full(wr), full(br)],
        out_specs=[tok(d), tok(d), tok(LANES)],
        out_shape=[jax.ShapeDtypeStruct((T, d), F32), jax.ShapeDtypeStruct((T, d), BF16),
                   jax.ShapeDtypeStruct((T, LANES), F32)],
        compiler_params=pltpu.CompilerParams(dimension_semantics=("arbitrary",),
                                             vmem_limit_bytes=VMEM_LIMIT),
        name="merge",
    )(x2d, o, yl, gatt, grec, woa, wol, wout, fg, wr, br)


def _moe_kernel(xn_ref, x1_ref, comb_ref, wg_ref, wu_ref, wd_ref, fin_ref, out_ref, acc_ref):
    e = pl.program_id(1)

    @pl.when(e == 0)
    def _():
        acc_ref[...] = jnp.zeros(acc_ref.shape, F32)

    xn = xn_ref[...]
    comb = comb_ref[...]
    lane = lax.broadcasted_iota(jnp.int32, comb.shape, 1)
    c_e = jnp.sum(jnp.where(lane == e, comb, 0.0), axis=-1, keepdims=True)
    gt = jnp.dot(xn, wg_ref[0], preferred_element_type=F32)
    up = jnp.dot(xn, wu_ref[0], preferred_element_type=F32)
    h = (gt * _sigmoid(gt)) * up * c_e
    acc_ref[...] += jnp.dot(h.astype(BF16), wd_ref[0], preferred_element_type=F32)

    @pl.when(e == pl.num_programs(1) - 1)
    def _():
        out_ref[...] = _rms(x1_ref[...] + acc_ref[...], fin_ref[...])


def _moe_call(xn2, x1, comb, wg, wu, wd, fin, tm):
    T, d = x1.shape
    E, _, de = wg.shape
    tok = lambda w: pl.BlockSpec((tm, w), lambda i, e: (i, 0))
    return pl.pallas_call(
        _moe_kernel,
        grid=(T // tm, E),
        in_specs=[tok(d), tok(d), tok(LANES),
                  pl.BlockSpec((1, d, de), lambda i, e: (e, 0, 0)),
                  pl.BlockSpec((1, d, de), lambda i, e: (e, 0, 0)),
                  pl.BlockSpec((1, de, d), lambda i, e: (e, 0, 0)),
                  pl.BlockSpec(fin.shape, lambda i, e: (0, 0))],
        out_specs=tok(d),
        out_shape=jax.ShapeDtypeStruct((T, d), F32),
        scratch_shapes=[pltpu.VMEM((tm, d), F32)],
        compiler_params=pltpu.CompilerParams(dimension_semantics=("arbitrary", "arbitrary"),
                                             vmem_limit_bytes=VMEM_LIMIT),
        name="moe",
    )(xn2, x1, comb, wg, wu, wd, fin)


def _rope_tables(S):
    inv = ROPE_THETA ** (-jnp.arange(0, QK_ROPE, 2, dtype=F32) / QK_ROPE)
    ang = jnp.arange(S, dtype=F32)[:, None] * inv[None, :]
    cos, sin = jnp.cos(ang), jnp.sin(ang)
    half = QK_ROPE // 2
    z = lambda n: jnp.zeros((S, n), F32)
    cos_t = jnp.concatenate([cos, cos, z(LANES - QK_ROPE)], axis=1)
    nsin_lo = jnp.concatenate([-sin, z(LANES - half)], axis=1)
    sin_hi = jnp.concatenate([z(half), sin, z(LANES - QK_ROPE)], axis=1)
    return cos_t, nsin_lo, sin_hi


def _prep_params(p):
    d = p["w_in"].shape[0]
    row = lambda a: a.reshape(1, -1).astype(F32)
    w_in = p["w_in"]
    c_q, c_kv, c_r = Q_LORA, Q_LORA + KV_LORA, Q_LORA + KV_LORA + QK_ROPE
    win = jnp.concatenate(
        [w_in[:, :c_r], jnp.zeros((d, LANES - QK_ROPE), F32), w_in[:, c_r:]], axis=1).astype(BF16)
    wqb = p["w_q_b"].reshape(Q_LORA, N_HEADS, QK_NOPE + QK_ROPE)
    wqb = jnp.pad(wqb, ((0, 0), (0, 0), (0, HEAD_PAD - QK_NOPE - QK_ROPE)))
    wqb = wqb.reshape(Q_LORA, N_HEADS * HEAD_PAD).astype(BF16)
    wkvb = p["w_kv_b"].reshape(KV_LORA, N_HEADS, QK_NOPE + V_HEAD)
    wkvb = jnp.concatenate([wkvb[:, :, :QK_NOPE].reshape(KV_LORA, -1),
                            wkvb[:, :, QK_NOPE:].reshape(KV_LORA, -1)], axis=1).astype(BF16)
    wlru = jnp.concatenate([p["lru_a_w"], p["lru_x_w"]], axis=-1).astype(BF16)
    wr = jnp.concatenate([p["router_expert_w"], p["router_group_w"],
                          jnp.zeros((d, LANES - N_EXPERTS - N_GROUPS), F32)], axis=1)
    wr_hi = wr.astype(BF16)
    wr = jnp.concatenate([wr_hi, (wr - wr_hi.astype(F32)).astype(BF16)], axis=1)
    br =jnp.concatenate([p["router_expert_b"], p["router_group_b"],
                          jnp.zeros((LANES - N_EXPERTS - N_GROUPS,), F32)]).reshape(1, LANES)
    return dict(
        mixg=row(p["mix_norm"]), win=win, qg=row(p["q_a_norm"]), wqb=wqb, kvg=row(p["kv_a_norm"]),
        wkvb=wkvb, gateb=p["gate_b"].astype(F32), cw=p["conv_w"].astype(F32), cb=row(p["conv_b"]),
        wlru=wlru, ab=p["lru_a_b"].astype(F32), xb=p["lru_x_b"].astype(F32),
        lam=p["lru_lambda"].astype(F32), woa=p["w_o_attn"].astype(BF16),
        wol=p["w_o_lru"].astype(BF16), wout=p["w_out"].astype(BF16), fg=row(p["ffn_norm"]),
        wr=wr, br=br, wg=p["w_gate_e"].astype(BF16), wu=p["w_up_e"].astype(BF16),
        wd=p["w_down_e"].astype(BF16), fin=row(p["final_norm"]))


def _trunk(x, w):
    B, S, d = x.shape
    T = B * S
    x2d = x.reshape(T, d)
    tm = min(256, S)
    tabs = _rope_tables(S)
    q, k, v, x_lru, gelu_g, gatt, grec = _proj_call(
        x2d, B, S, tabs, w["mixg"], w["win"], w["qg"], w["wqb"], w["kvg"], w["wkvb"], w["gateb"], tm)
    o = _attn_call(q, k, v, tq=min(1024, S), tk=min(1024, S), unroll=4)

    L = min(256, S)
    x_lru3 = x_lru.reshape(B, S, d)
    gelu3 = gelu_g.reshape(B, S, d)
    hb = _lru_call(x_lru3, w["cw"], w["cb"], w["wlru"][1], w["ab"][1:2], w["xb"][1:2],
                   w["lam"][1:2], None, None, L, reverse=True)
    yl = _lru_call(x_lru3, w["cw"], w["cb"], w["wlru"][0], w["ab"][0:1], w["xb"][0:1],
                   w["lam"][0:1], hb, gelu3, L, reverse=False)

    x1, xn2, comb = _merge_call(x2d, o.reshape(T, d), yl.reshape(T, d), gatt, grec, w["woa"],
                                w["wol"], w["wout"], w["fg"], w["wr"], w["br"], tm=min(512, S))
    y = _moe_call(xn2, x1, comb, w["wg"], w["wu"], w["wd"], w["fin"], tm=min(1024, S))
    return y.reshape(B, S, d)


def kernel(x_prompt, x_sample, mix_norm, w_in, q_a_norm, w_q_b, kv_a_norm, w_kv_b, w_o_attn, conv_w, conv_b, lru_a_w, lru_a_b, lru_x_w, lru_x_b, lru_lambda, w_o_lru, gate_b, w_out, ffn_norm, router_group_w, router_group_b, router_expert_w, router_expert_b, w_gate_e, w_up_e, w_down_e, final_norm):
    depth = mix_norm.shape[0]
    layers = []
    for l in range(depth):
        layers.append(_prep_params(dict(
            mix_norm=mix_norm[l], w_in=w_in[l], q_a_norm=q_a_norm[l], w_q_b=w_q_b[l],
            kv_a_norm=kv_a_norm[l], w_kv_b=w_kv_b[l], w_o_attn=w_o_attn[l], conv_w=conv_w[l],
            conv_b=conv_b[l], lru_a_w=lru_a_w[l], lru_a_b=lru_a_b[l], lru_x_w=lru_x_w[l],
            lru_x_b=lru_x_b[l], lru_lambda=lru_lambda[l], w_o_lru=w_o_lru[l], gate_b=gate_b[l],
            w_out=w_out[l], ffn_norm=ffn_norm[l], router_group_w=router_group_w[l],
            router_group_b=router_group_b[l], router_expert_w=router_expert_w[l],
            router_expert_b=router_expert_b[l], w_gate_e=w_gate_e[l], w_up_e=w_up_e[l],
            w_down_e=w_down_e[l], final_norm=final_norm)))
    assert depth == 1

    def trunk(x):
        for w in layers:
            x = _trunk(x, w)
        return x

    return (trunk(x_prompt), trunk(x_sample))
```

```python
import functools
import math

import jax
import jax.numpy as jnp
from jax import lax
from jax.experimental import pallas as pl
from jax.experimental.pallas import tpu as pltpu

N_HEADS = 8
QK_NOPE = 128
QK_ROPE = 64
V_HEAD = 128
Q_LORA = 384
KV_LORA = 256
ROPE_THETA = 10000.0
LRU_BLOCKS = 8
LRU_C = 8.0
N_GROUPS = 4
EXPERTS_PER_GROUP = 8
N_EXPERTS = N_GROUPS * EXPERTS_PER_GROUP
EPS = 1e-6

LANES = 128
HEAD_PAD = 256
LAT_COLS = Q_LORA + KV_LORA + LANES
PAIR_SLOTS = EXPERTS_PER_GROUP * EXPERTS_PER_GROUP
N_BUCKETS = N_GROUPS * PAIR_SLOTS
ROW_X = 8
ROW_XW = 16
DMA_GROUP = 16
VMEM_LIMIT = 56 * 1024 * 1024

F32 = jnp.float32
BF16 = jnp.bfloat16


def _sigmoid(z):
    return 1.0 / (1.0 + jnp.exp(-z))


def _rms(x, g):
    var = jnp.mean(x * x, axis=-1, keepdims=True)
    return x * lax.rsqrt(var + EPS) * g


def _rope(pe, cos_t, nsin_lo, sin_hi):
    x2_to_lo = pltpu.roll(pe, 96, axis=1)
    x1_to_hi = pltpu.roll(pe, 32, axis=1)
    return pe * cos_t + x2_to_lo * nsin_lo + x1_to_hi * sin_hi


def _proj_kernel(x_ref, cos_ref, nsl_ref, sh_ref, mixg_ref, win_ref, qg_ref, wqb_ref, kvg_ref,
                 wkvb_ref, gateb_ref, q_ref, k_ref, v_ref, xlru_ref, gelu_ref, gatt_ref,
                 grec_ref, *, scale):
    d = x_ref.shape[-1]
    xn = _rms(x_ref[...], mixg_ref[...]).astype(BF16)
    lat = jnp.dot(xn, win_ref[:, 0:LAT_COLS], preferred_element_type=F32)
    cos_t, nsl, sh = cos_ref[...], nsl_ref[...], sh_ref[...]

    qn = _rms(lat[:, 0:Q_LORA], qg_ref[...]).astype(BF16)
    qf = jnp.dot(qn, wqb_ref[...], preferred_element_type=F32)
    for h in range(N_HEADS):
        base = h * HEAD_PAD
        q_ref[0, h, :, 0:LANES] = (qf[:, base:base + LANES] * scale).astype(BF16)
        pe = _rope(qf[:, base + LANES:base + HEAD_PAD], cos_t, nsl, sh)
        q_ref[0, h, :, LANES:HEAD_PAD] = (pe * scale).astype(BF16)

    kvn = _rms(lat[:, Q_LORA:Q_LORA + KV_LORA], kvg_ref[...]).astype(BF16)
    kvf = jnp.dot(kvn, wkvb_ref[...], preferred_element_type=F32)
    kpe = _rope(lat[:, Q_LORA + KV_LORA:LAT_COLS], cos_t, nsl, sh).astype(BF16)
    for h in range(N_HEADS):
        k_ref[0, h, :, 0:LANES] = kvf[:, h * LANES:(h + 1) * LANES].astype(BF16)
        k_ref[0, h, :, LANES:HEAD_PAD] = kpe
        v_ref[0, h, :, :] = kvf[:, (N_HEADS + h) * LANES:(N_HEADS + h + 1) * LANES].astype(BF16)

    c0 = LAT_COLS
    xlru_ref[...] = jnp.dot(xn, win_ref[:, c0:c0 + d], preferred_element_type=F32)
    g = jnp.dot(xn, win_ref[:, c0 + d:c0 + 2 * d], preferred_element_type=F32)
    gelu_ref[...] = jax.nn.gelu(g).astype(BF16)
    ga = jnp.dot(xn, win_ref[:, c0 + 2 * d:c0 + 3 * d], preferred_element_type=F32)
    gatt_ref[...] = _sigmoid(ga + gateb_ref[0:1, :]).astype(BF16)
    gb = jnp.dot(xn, win_ref[:, c0 + 3 * d:c0 + 4 * d], preferred_element_type=F32)
    grec_ref[...] = _sigmoid(gb + gateb_ref[1:2, :]).astype(BF16)


def _proj_call(x2d, B, S, tabs, mixg, win, qg, wqb, kvg, wkvb, gateb, tm):
    T, d = x2d.shape
    nt_seq = S // tm
    full = lambda a: pl.BlockSpec(a.shape, lambda i: (0,) * a.ndim)
    tab_spec = pl.BlockSpec((tm, LANES), lambda i: (i % nt_seq, 0))
    tok = lambda w: pl.BlockSpec((tm, w), lambda i: (i, 0))
    head_spec = lambda w: pl.BlockSpec((1, N_HEADS, tm, w), lambda i: (i // nt_seq, 0, i % nt_seq, 0))
    scale = math.log2(math.e) / math.sqrt(QK_NOPE + QK_ROPE)
    return pl.pallas_call(
        functools.partial(_proj_kernel, scale=scale),
        grid=(T // tm,),
        in_specs=[tok(d), tab_spec, tab_spec, tab_spec, full(mixg), full(win), full(qg), full(wqb),
                  full(kvg), full(wkvb), full(gateb)],
        out_specs=[head_spec(HEAD_PAD), head_spec(HEAD_PAD), head_spec(V_HEAD),
                   tok(d), tok(d), tok(d), tok(d)],
        out_shape=[
            jax.ShapeDtypeStruct((B, N_HEADS, S, HEAD_PAD), BF16),
            jax.ShapeDtypeStruct((B, N_HEADS, S, HEAD_PAD), BF16),
            jax.ShapeDtypeStruct((B, N_HEADS, S, V_HEAD), BF16),
            jax.ShapeDtypeStruct((T, d), F32),
            jax.ShapeDtypeStruct((T, d), BF16),
            jax.ShapeDtypeStruct((T, d), BF16),
            jax.ShapeDtypeStruct((T, d), BF16),
        ],
        compiler_params=pltpu.CompilerParams(dimension_semantics=("arbitrary",),
                                             vmem_limit_bytes=VMEM_LIMIT),
        name="proj",
    )(x2d, *tabs, mixg, win, qg, wqb, kvg, wkvb, gateb)


def _attn_kernel(q_ref, k_ref, v_ref, o_ref, s_ref, m_ref, l_ref, acc_ref, *, tk, unroll):
    q = q_ref[0, 0]
    nk = k_ref.shape[2] // tk
    n_rep = tk // LANES

    def scores(j):
        off = j * tk if isinstance(j, int) else pl.multiple_of(j * tk, tk)
        kc = k_ref[0, 0, pl.ds(off, tk), :]
        return lax.dot_general(q, kc, (((1,), (1,)), ((), ())), preferred_element_type=F32)

    def consume(j, s):
        off = j * tk if isinstance(j, int) else pl.multiple_of(j * tk, tk)
        m_old = m_ref[...]
        m_new = jnp.maximum(m_old, jnp.max(s, axis=-1, keepdims=True))
        alpha = jnp.exp2(m_old - m_new)
        p = jnp.exp2(s - jnp.tile(m_new, (1, n_rep)))
        psum = p[:, 0:LANES]
        for c in range(1, n_rep):
            psum = psum + p[:, c * LANES:(c + 1) * LANES]
        l_ref[...] = alpha * l_ref[...] + psum
        pv = jnp.dot(p.astype(BF16), v_ref[0, 0, pl.ds(off, tk), :], preferred_element_type=F32)
        acc_ref[...] = alpha * acc_ref[...] + pv
        m_ref[...] = m_new

    m_ref[...] = jnp.full(m_ref.shape, -jnp.inf, F32)
    l_ref[...] = jnp.zeros(l_ref.shape, F32)
    acc_ref[...] = jnp.zeros(acc_ref.shape, F32)
    s_ref[0] = scores(0)

    def step(j, cur):
        s = s_ref[cur]
        s_ref[1 - cur] = scores(j + 1)
        consume(j, s)

    if nk <= unroll:
        for j in range(nk - 1):
            step(j, j % 2)
    else:
        assert unroll % 2 == 0 and nk % unroll == 0
        def group(g, _):
            for u in range(unroll):
                step(unroll * g + u, u % 2)
            return 0
        lax.fori_loop(0, nk // unroll - 1, group, 0)
        for j in range(nk - unroll, nk - 1):
            step(j, j % 2)
    consume(nk - 1, s_ref[(nk - 1) % 2])
    l = jnp.sum(l_ref[...], axis=-1, keepdims=True)
    o_ref[0] = (acc_ref[...] / l).astype(BF16)


def _attn_call(q, k, v, tq, tk, unroll=8):
    B, H, S, _ = q.shape
    return pl.pallas_call(
        functools.partial(_attn_kernel, tk=tk, unroll=unroll),
        grid=(B, H, S // tq),
        in_specs=[
            pl.BlockSpec((1, 1, tq, HEAD_PAD), lambda b, h, i: (b, h, i, 0)),
            pl.BlockSpec((1, 1, S, HEAD_PAD), lambda b, h, i: (b, h, 0, 0)),
            pl.BlockSpec((1, 1, S, V_HEAD), lambda b, h, i: (b, h, 0, 0)),
        ],
        out_specs=pl.BlockSpec((1, tq, V_HEAD), lambda b, h, i: (b, i, h)),
        out_shape=jax.ShapeDtypeStruct((B, S, H * V_HEAD), BF16),
        scratch_shapes=[pltpu.VMEM((2, tq, tk), F32), pltpu.VMEM((tq, LANES), F32),
                        pltpu.VMEM((tq, LANES), F32), pltpu.VMEM((tq, V_HEAD), F32)],
        compiler_params=pltpu.CompilerParams(
            dimension_semantics=("arbitrary", "arbitrary", "arbitrary"),
            vmem_limit_bytes=VMEM_LIMIT),
        name="attn",
    )(q, k, v)


def _lru_kernel(*refs, reverse, n_chunks):
    if reverse:
        (x_ref, xp_ref, xn_ref, cw_ref, cb_ref, w_ref, ab_ref, xb_ref, lam_ref,
         out_ref, h_ref, a_s, b_s, h_s) = refs
    else:
        (x_ref, xp_ref, xn_ref, cw_ref, cb_ref, w_ref, ab_ref, xb_ref, lam_ref, hb_ref, gelu_ref,
         out_ref, h_ref, a_s, b_s, h_s) = refs
    j = pl.program_id(1)
    c = (n_chunks - 1 - j) if reverse else j
    L, W = x_ref.shape[1], x_ref.shape[2]
    blk = W // LRU_BLOCKS

    @pl.when(j == 0)
    def _():
        h_ref[...] = jnp.zeros(h_ref.shape, F32)

    x = x_ref[0]
    prev = jnp.where(c > 0, xp_ref[0], 0.0)
    nxt = jnp.where(c < n_chunks - 1, xn_ref[0], 0.0)
    row = lax.broadcasted_iota(jnp.int32, (L, W), 0)
    xm1 = jnp.where(row == 0, prev[7:8, :], pltpu.roll(x, 1, axis=0))
    xm2 = jnp.where(row == 0, prev[6:7, :], jnp.where(row == 1, prev[7:8, :],
                                                       pltpu.roll(x, 2, axis=0)))
    xp1 = jnp.where(row == L - 1, nxt[0:1, :], pltpu.roll(x, L - 1, axis=0))
    xc = cb_ref[...] + (xm2 * cw_ref[0:1, :] + xm1 * cw_ref[1:2, :] + x * cw_ref[2:3, :]
                        + xp1 * cw_ref[3:4, :])
    xcb = xc.astype(BF16)
    sp = jnp.log1p(jnp.exp(-jnp.abs(lam_ref[...]))) + jnp.maximum(-lam_ref[...], 0.0)
    for n in range(LRU_BLOCKS):
        sl = slice(n * blk, (n + 1) * blk)
        z = jnp.dot(xcb[:, sl], w_ref[n], preferred_element_type=F32)
        r = _sigmoid(z[:, 0:blk] + ab_ref[:, sl])
        i = _sigmoid(z[:, blk:2 * blk] + xb_ref[:, sl])
        log_a = (-LRU_C) * r * sp[:, sl]
        a = jnp.exp(log_a)
        a_s[:, sl] = a
        b_s[:, sl] = jnp.sqrt(1.0 - a * a) * (i * xc[:, sl])

    def step(t, h):
        tt = (L - 1 - t) if reverse else t
        h = a_s[pl.ds(tt, 1), :] * h + b_s[pl.ds(tt, 1), :]
        h_s[pl.ds(tt, 1), :] = h
        return h

    h_ref[...] = lax.fori_loop(0, L, step, h_ref[...], unroll=8)
    if reverse:
        out_ref[0] = h_s[...].astype(BF16)
    else:
        hsum = h_s[...] + hb_ref[0].astype(F32)
        out_ref[0] = (hsum * gelu_ref[0].astype(F32)).astype(BF16)


def _lru_call(x_lru, cw, cb, w, ab, xb, lam, hb, gelu, L, reverse):
    B, S, W = x_lru.shape
    nC = S // L
    pos = (lambda j: nC - 1 - j) if reverse else (lambda j: j)
    r8 = L // 8
    full = lambda a: pl.BlockSpec(a.shape, lambda b, j: (0,) * a.ndim)
    chunk = pl.BlockSpec((1, L, W), lambda b, j: (b, pos(j), 0))
    in_specs = [
        chunk,
        pl.BlockSpec((1, 8, W), lambda b, j: (b, jnp.maximum(pos(j) * r8 - 1, 0), 0)),
        pl.BlockSpec((1, 8, W), lambda b, j: (b, jnp.minimum((pos(j) + 1) * r8, S // 8 - 1), 0)),
        full(cw), full(cb), full(w), full(ab), full(xb), full(lam),
    ]
    args = [x_lru, x_lru, x_lru, cw, cb, w, ab, xb, lam]
    if not reverse:
        in_specs += [chunk, chunk]
        args += [hb, gelu]
    return pl.pallas_call(
        functools.partial(_lru_kernel, reverse=reverse, n_chunks=nC),
        grid=(B, nC),
        in_specs=in_specs,
        out_specs=chunk,
        out_shape=jax.ShapeDtypeStruct((B, S, W), BF16),
        scratch_shapes=[pltpu.VMEM((1, W), F32), pltpu.VMEM((L, W), F32), pltpu.VMEM((L, W), F32),
                        pltpu.VMEM((L, W), F32)],
        compiler_params=pltpu.CompilerParams(dimension_semantics=("arbitrary", "arbitrary"),
                                             vmem_limit_bytes=VMEM_LIMIT),
        name="lru_bwd" if reverse else "lru_fwd",
    )(*args)


def _merge_kernel(x_ref, o_ref, yl_ref, gatt_ref, grec_ref, woa_ref, wol_ref, wout_ref, fg_ref,
                  wr_ref, br_ref, xw_ref, meta_ref, cnt_ref):
    tm = x_ref.shape[0]
    y_attn = jnp.dot(o_ref[...], woa_ref[...], preferred_element_type=F32)
    y_lru = jnp.dot(yl_ref[...], wol_ref[...], preferred_element_type=F32)
    mix = gatt_ref[...].astype(F32) * y_attn + grec_ref[...].astype(F32) * y_lru
    x1 = x_ref[...] + jnp.dot(mix.astype(BF16), wout_ref[...], preferred_element_type=F32)
    xn2 = _rms(x1, fg_ref[...])

    x_hi = xn2.astype(BF16)
    x_lo = (xn2 - x_hi.astype(F32)).astype(BF16)
    hh = jnp.dot(x_hi, wr_ref[...], preferred_element_type=F32)
    lh = jnp.dot(x_lo, wr_ref[:, 0:LANES], preferred_element_type=F32)
    logits = hh[:, 0:LANES] + (hh[:, LANES:2 * LANES] + lh) + br_ref[...]
    lane = lax.broadcasted_iota(jnp.int32, logits.shape, 1)
    big = jnp.int32(LANES)
    neg = jnp.float32(-jnp.inf)
    is_g = (lane >= N_EXPERTS) & (lane < N_EXPERTS + N_GROUPS)
    gl = jnp.where(is_g, logits, neg)
    gmax = jnp.max(gl, axis=-1, keepdims=True)
    g_p = 1.0 / jnp.sum(jnp.exp(gl - gmax), axis=-1, keepdims=True)
    g_idx = jnp.min(jnp.where(gl == gmax, lane - N_EXPERTS, big), axis=-1, keepdims=True)
    in_grp = (lane < N_EXPERTS) & ((lane // EXPERTS_PER_GROUP) == g_idx)
    el = jnp.where(in_grp, logits, neg)
    m1 = jnp.max(el, axis=-1, keepdims=True)
    i1 = jnp.min(jnp.where(el == m1, lane, big), axis=-1, keepdims=True)
    el2 = jnp.where(lane == i1, neg, el)
    m2 = jnp.max(el2, axis=-1, keepdims=True)
    i2 = jnp.min(jnp.where(el2 == m2, lane, big), axis=-1, keepdims=True)
    e2 = jnp.exp(m2 - m1)
    w1 = g_p / (1.0 + e2)
    w2 = w1 * e2

    first_lo = i1 < i2
    e_lo = jnp.where(first_lo, i1, i2)
    e_hi = jnp.where(first_lo, i2, i1)
    w_lo = jnp.where(first_lo, w1, w2)
    w_hi = jnp.where(first_lo, w2, w1)
    bucket = g_idx * PAIR_SLOTS + (e_lo % EXPERTS_PER_GROUP) * EXPERTS_PER_GROUP + e_hi % EXPERTS_PER_GROUP

    @pl.when(pl.program_id(0) == 0)
    def _():
        cnt_ref[...] = jnp.zeros(cnt_ref.shape, F32)

    blane = lax.broadcasted_iota(jnp.int32, (tm, N_BUCKETS), 1)
    onehot = blane == bucket
    tri = (lax.broadcasted_iota(jnp.int32, (tm, tm), 0)
           > lax.broadcasted_iota(jnp.int32, (tm, tm), 1))
    earlier = jnp.dot(tri.astype(BF16), onehot.astype(BF16), preferred_element_type=F32)
    base = cnt_ref[0:1, :]
    rank = jnp.sum(jnp.where(onehot, earlier + base, 0.0), axis=-1, keepdims=True)
    cnt_ref[0:1, :] = base + jnp.sum(onehot.astype(F32), axis=0, keepdims=True)
    meta_ref[...] = jnp.where(lane == 0, bucket, jnp.where(lane == 1, rank.astype(jnp.int32), 0))

    for c in range(ROW_X):
        xw_ref[pl.ds(c, tm, stride=ROW_XW), :] = x1[:, c * LANES:(c + 1) * LANES]
    xw_ref[pl.ds(ROW_X, tm, stride=ROW_XW), :] = jnp.where(lane == 0, w_lo,
                                                        jnp.where(lane == 1, w_hi, 0.0))
    zeros = jnp.zeros((tm, LANES), F32)
    for c in range(ROW_X + 1, ROW_XW):
        xw_ref[pl.ds(c, tm, stride=ROW_XW), :] = zeros


def _merge_call(x2d, o, yl, gatt, grec, woa, wol, wout, fg, wr, br, tm):
    T, d = x2d.shape
    assert d == ROW_X * LANES
    full = lambda a: pl.BlockSpec(a.shape, lambda i: (0,) * a.ndim)
    tok = lambda w: pl.BlockSpec((tm, w), lambda i: (i, 0))
    return pl.pallas_call(
        _merge_kernel,
        grid=(T // tm,),
        in_specs=[tok(d), tok(d), tok(d), tok(d), tok(d), full(woa), full(wol), full(wout),
                  full(fg), full(wr), full(br)],
        out_specs=[pl.BlockSpec((tm * ROW_XW, LANES), lambda i: (i, 0)), tok(LANES),
                   pl.BlockSpec((8, N_BUCKETS), lambda i: (0, 0))],
        out_shape=[jax.ShapeDtypeStruct((T * ROW_XW, LANES), F32),
                   jax.ShapeDtypeStruct((T, LANES), jnp.int32),
                   jax.ShapeDtypeStruct((8, N_BUCKETS), F32)],
        compiler_params=pltpu.CompilerParams(dimension_semantics=("arbitrary",),
                                             vmem_limit_bytes=VMEM_LIMIT),
        name="merge",
    )(x2d, o, yl, gatt, grec, woa, wol, wout, fg, wr, br)


def _row_copy(src, src_row, dst, dst_row, rows, sem):
    return pltpu.make_async_copy(src.at[pl.ds(src_row * rows, rows)],
                                 dst.at[pl.ds(dst_row * rows, rows)], sem)


def _for_row_groups(n_rows, slot_of, copy_of):
    def start(g, _):
        rows = [g * DMA_GROUP + u for u in range(DMA_GROUP)]
        slots = [slot_of(t) for t in rows]
        for t, s in zip(rows, slots):
            copy_of(t, s).start()
        return 0

    def wait(g, _):
        for u in range(DMA_GROUP):
            copy_of(g * DMA_GROUP + u, 0).wait()
        return 0

    lax.fori_loop(0, n_rows // DMA_GROUP, start, 0)
    lax.fori_loop(0, n_rows // DMA_GROUP, wait, 0)


def _fill_unused_slots(cnt_ref, off_ref, total_ref, zero_ref, xs_ref, sem, R, n_tiles):
    zero_ref[...] = jnp.zeros(zero_ref.shape, F32)
    bits = R.bit_length() - 1

    def run_copy(pos, size):
        return pltpu.make_async_copy(zero_ref.at[pl.ds(0, size * ROW_XW)],
                                     xs_ref.at[pl.ds(pos * ROW_XW, size * ROW_XW)], sem)

    def bucket_pads(b, op):
        pad = (-cnt_ref[b]) & (R - 1)
        pos = off_ref[b] + cnt_ref[b]
        for k in range(bits):
            has = (pad >> k) & 1

            @pl.when(has == 1)
            def _():
                op(run_copy(pos, 1 << k))

            pos = pos + has * (1 << k)

    def tail_tile(i, op):
        op(run_copy(i * R, R))

    for op in (lambda c: c.start(), lambda c: c.wait()):
        lax.fori_loop(0, N_BUCKETS, lambda b, _: (bucket_pads(b, op), 0)[1], 0)
        lax.fori_loop(total_ref[0], n_tiles, lambda i, _: (tail_tile(i, op), 0)[1], 0)


def _dispatch_kernel(bucket_ref, rank_ref, off_ref, cnt_ref, total_ref, xw_ref, xs_ref, zero_ref,
                     sem, fill_sem, *, tb, R, n_tiles):
    base = pl.program_id(0) * tb
    _for_row_groups(
        tb,
        lambda t: off_ref[bucket_ref[t]] + rank_ref[t],
        lambda t, slot: _row_copy(xw_ref, base + t, xs_ref, slot, ROW_XW, sem))

    @pl.when(pl.program_id(0) == 0)
    def _():
        _fill_unused_slots(cnt_ref, off_ref, total_ref, zero_ref, xs_ref, fill_sem, R, n_tiles)


def _dispatch_call(bucket, rank, off, cnt, total, xw, n_tiles, R, tb):
    T = bucket.shape[0]
    smem_blk = pl.BlockSpec((tb,), lambda i: (i,), memory_space=pltpu.SMEM)
    smem = pl.BlockSpec(memory_space=pltpu.SMEM)
    return pl.pallas_call(
        functools.partial(_dispatch_kernel, tb=tb, R=R, n_tiles=n_tiles),
        grid=(T // tb,),
        in_specs=[smem_blk, smem_blk, smem, smem, smem, pl.BlockSpec(memory_space=pl.ANY)],
        out_specs=pl.BlockSpec(memory_space=pl.ANY),
        out_shape=jax.ShapeDtypeStruct((n_tiles * R * ROW_XW, LANES), F32),
        scratch_shapes=[pltpu.VMEM((R * ROW_XW, LANES), F32), pltpu.SemaphoreType.DMA(()),
                        pltpu.SemaphoreType.DMA(())],
        compiler_params=pltpu.CompilerParams(dimension_semantics=("arbitrary",)),
        name="dispatch",
    )(bucket, rank, off, cnt, total, xw)


def _expert_kernel(lo_ref, hi_ref, nv_ref, blk_ref, xs_ref, fg_ref, wgl_ref, wul_ref, wdl_ref,
                   wgh_ref, wuh_ref, wdh_ref, fin_ref, ys_ref, *, R):
    nv = nv_ref[pl.program_id(0)]

    @pl.when(nv == 0)
    def _():
        ys_ref[...] = jnp.zeros(ys_ref.shape, F32)

    @pl.when(nv > 0)
    def _():
        x1 = jnp.concatenate([xs_ref[pl.ds(c, R, stride=ROW_XW), :] for c in range(ROW_X)], axis=1)
        wrow = xs_ref[pl.ds(ROW_X, R, stride=ROW_XW), :]
        lane = lax.broadcasted_iota(jnp.int32, wrow.shape, 1)
        w_lo = jnp.sum(jnp.where(lane == 0, wrow, 0.0), axis=-1, keepdims=True)
        w_hi = jnp.sum(jnp.where(lane == 1, wrow, 0.0), axis=-1, keepdims=True)
        xn = _rms(x1, fg_ref[...]).astype(BF16)

        def hidden(wg_ref, wu_ref, w):
            gt = jnp.dot(xn, wg_ref[0], preferred_element_type=F32)
            up = jnp.dot(xn, wu_ref[0], preferred_element_type=F32)
            return ((gt * _sigmoid(gt)) * up * w).astype(BF16)

        moe = (jnp.dot(hidden(wgl_ref, wul_ref, w_lo), wdl_ref[0], preferred_element_type=F32)
               + jnp.dot(hidden(wgh_ref, wuh_ref, w_hi), wdh_ref[0], preferred_element_type=F32))
        y = _rms(x1 + moe, fin_ref[...])
        for c in range(ROW_X):
            ys_ref[pl.ds(c, R, stride=ROW_X), :] = y[:, c * LANES:(c + 1) * LANES]


def _expert_call(tile_lo, tile_hi, tile_nv, tile_blk, xs, fg, wg, wu, wd, fin, R):
    n_tiles = tile_lo.shape[0]
    E, d, de = wg.shape
    lo_map = lambda i, lo, hi, nv, blk: (lo[i], 0, 0)
    hi_map = lambda i, lo, hi, nv, blk: (hi[i], 0, 0)
    const = lambda a: pl.BlockSpec(a.shape, lambda i, lo, hi, nv, blk: (0,) * a.ndim)
    grid_spec = pltpu.PrefetchScalarGridSpec(
        num_scalar_prefetch=4,
        grid=(n_tiles,),
        in_specs=[pl.BlockSpec((R * ROW_XW, LANES), lambda i, lo, hi, nv, blk: (blk[i], 0)),
                  const(fg),
                  pl.BlockSpec((1, d, de), lo_map), pl.BlockSpec((1, d, de), lo_map),
                  pl.BlockSpec((1, de, d), lo_map),
                  pl.BlockSpec((1, d, de), hi_map), pl.BlockSpec((1, d, de), hi_map),
                  pl.BlockSpec((1, de, d), hi_map),
                  const(fin)],
        out_specs=pl.BlockSpec((R * ROW_X, LANES), lambda i, lo, hi, nv, blk: (i, 0)),
    )
    return pl.pallas_call(
        functools.partial(_expert_kernel, R=R),
        grid_spec=grid_spec,
        out_shape=jax.ShapeDtypeStruct((n_tiles * R * ROW_X, LANES), F32),
        compiler_params=pltpu.CompilerParams(dimension_semantics=("arbitrary",),
                                             vmem_limit_bytes=VMEM_LIMIT),
        name="experts",
    )(tile_lo, tile_hi, tile_nv, tile_blk, xs, fg, wg, wu, wd, wg, wu, wd, fin)


def _return_kernel(bucket_ref, rank_ref, off_ref, ys_ref, y_ref, buf_ref, sem, *, tb):
    _for_row_groups(
        tb,
        lambda t: off_ref[bucket_ref[t]] + rank_ref[t],
        lambda t, slot: _row_copy(ys_ref, slot, buf_ref, t, ROW_X, sem))
    for c in range(ROW_X):
        y_ref[:, c * LANES:(c + 1) * LANES] = buf_ref[pl.ds(c, tb, stride=ROW_X), :]


def _return_call(bucket, rank, off, ys, d, tb):
    T = bucket.shape[0]
    smem_blk = pl.BlockSpec((tb,), lambda i: (i,), memory_space=pltpu.SMEM)
    return pl.pallas_call(
        functools.partial(_return_kernel, tb=tb),
        grid=(T // tb,),
        in_specs=[smem_blk, smem_blk, pl.BlockSpec(memory_space=pltpu.SMEM),
                  pl.BlockSpec(memory_space=pl.ANY)],
        out_specs=pl.BlockSpec((tb, d), lambda i: (i, 0)),
        out_shape=jax.ShapeDtypeStruct((T, d), F32),
        scratch_shapes=[pltpu.VMEM((tb * ROW_X, LANES), F32), pltpu.SemaphoreType.DMA(())],
        compiler_params=pltpu.CompilerParams(dimension_semantics=("arbitrary",),
                                             vmem_limit_bytes=VMEM_LIMIT),
        name="moe_return",
    )(bucket, rank, off, ys)


def _routing_tables(counts, T, R):
    n_tiles = T // R + N_GROUPS * (EXPERTS_PER_GROUP * (EXPERTS_PER_GROUP - 1) // 2)
    cnt = counts.astype(jnp.int32)
    tiles_b = (cnt + R - 1) // R
    tile_end = jnp.cumsum(tiles_b)
    tile_start = tile_end - tiles_b
    off = tile_start * R
    total = tile_end[-1]
    i = jnp.arange(n_tiles, dtype=jnp.int32)
    blk = jnp.minimum(i, total - 1)
    b = jnp.searchsorted(tile_end, blk, side="right").astype(jnp.int32)
    nv = jnp.where(i < total, jnp.clip(cnt[b] - (blk - tile_start[b]) * R, 0, R), 0)
    grp = (b // PAIR_SLOTS) * EXPERTS_PER_GROUP
    lo = grp + (b % PAIR_SLOTS) // EXPERTS_PER_GROUP
    hi = grp + b % EXPERTS_PER_GROUP
    return cnt, off, total.reshape(1), lo, hi, nv.astype(jnp.int32), blk, n_tiles


def _rope_tables(S):
    inv = ROPE_THETA ** (-jnp.arange(0, QK_ROPE, 2, dtype=F32) / QK_ROPE)
    ang = jnp.arange(S, dtype=F32)[:, None] * inv[None, :]
    cos, sin = jnp.cos(ang), jnp.sin(ang)
    half = QK_ROPE // 2
    z = lambda n: jnp.zeros((S, n), F32)
    cos_t = jnp.concatenate([cos, cos, z(LANES - QK_ROPE)], axis=1)
    nsin_lo = jnp.concatenate([-sin, z(LANES - half)], axis=1)
    sin_hi = jnp.concatenate([z(half), sin, z(LANES - QK_ROPE)], axis=1)
    return cos_t, nsin_lo, sin_hi


def _prep_params(p):
    d = p["w_in"].shape[0]
    row = lambda a: a.reshape(1, -1).astype(F32)
    w_in = p["w_in"]
    c_q, c_kv, c_r = Q_LORA, Q_LORA + KV_LORA, Q_LORA + KV_LORA + QK_ROPE
    win = jnp.concatenate(
        [w_in[:, :c_r], jnp.zeros((d, LANES - QK_ROPE), F32), w_in[:, c_r:]], axis=1).astype(BF16)
    wqb = p["w_q_b"].reshape(Q_LORA, N_HEADS, QK_NOPE + QK_ROPE)
    wqb = jnp.pad(wqb, ((0, 0), (0, 0), (0, HEAD_PAD - QK_NOPE - QK_ROPE)))
    wqb = wqb.reshape(Q_LORA, N_HEADS * HEAD_PAD).astype(BF16)
    wkvb = p["w_kv_b"].reshape(KV_LORA, N_HEADS, QK_NOPE + V_HEAD)
    wkvb = jnp.concatenate([wkvb[:, :, :QK_NOPE].reshape(KV_LORA, -1),
                            wkvb[:, :, QK_NOPE:].reshape(KV_LORA, -1)], axis=1).astype(BF16)
    wlru = jnp.concatenate([p["lru_a_w"], p["lru_x_w"]], axis=-1).astype(BF16)
    wr = jnp.concatenate([p["router_expert_w"], p["router_group_w"],
                          jnp.zeros((d, LANES - N_EXPERTS - N_GROUPS), F32)], axis=1)
    wr_hi = wr.astype(BF16)
    wr = jnp.concatenate([wr_hi, (wr - wr_hi.astype(F32)).astype(BF16)], axis=1)
    br =jnp.concatenate([p["router_expert_b"], p["router_group_b"],
                          jnp.zeros((LANES - N_EXPERTS - N_GROUPS,), F32)]).reshape(1, LANES)
    return dict(
        mixg=row(p["mix_norm"]), win=win, qg=row(p["q_a_norm"]), wqb=wqb, kvg=row(p["kv_a_norm"]),
        wkvb=wkvb, gateb=p["gate_b"].astype(F32), cw=p["conv_w"].astype(F32), cb=row(p["conv_b"]),
        wlru=wlru, ab=p["lru_a_b"].astype(F32), xb=p["lru_x_b"].astype(F32),
        lam=p["lru_lambda"].astype(F32), woa=p["w_o_attn"].astype(BF16),
        wol=p["w_o_lru"].astype(BF16), wout=p["w_out"].astype(BF16), fg=row(p["ffn_norm"]),
        wr=wr, br=br, wg=p["w_gate_e"].astype(BF16), wu=p["w_up_e"].astype(BF16),
        wd=p["w_down_e"].astype(BF16), fin=row(p["final_norm"]))


def _trunk(x, w):
    B, S, d = x.shape
    T = B * S
    x2d = x.reshape(T, d)
    tm = min(256, S)
    tabs = _rope_tables(S)
    q, k, v, x_lru, gelu_g, gatt, grec = _proj_call(
        x2d, B, S, tabs, w["mixg"], w["win"], w["qg"], w["wqb"], w["kvg"], w["wkvb"], w["gateb"], tm)
    o = _attn_call(q, k, v, tq=min(1024, S), tk=min(1024, S), unroll=4)

    L = min(256, S)
    x_lru3 = x_lru.reshape(B, S, d)
    gelu3 = gelu_g.reshape(B, S, d)
    hb = _lru_call(x_lru3, w["cw"], w["cb"], w["wlru"][1], w["ab"][1:2], w["xb"][1:2],
                   w["lam"][1:2], None, None, L, reverse=True)
    yl = _lru_call(x_lru3, w["cw"], w["cb"], w["wlru"][0], w["ab"][0:1], w["xb"][0:1],
                   w["lam"][0:1], hb, gelu3, L, reverse=False)

    xw, meta, cnt = _merge_call(x2d, o.reshape(T, d), yl.reshape(T, d), gatt, grec, w["woa"],
                                w["wol"], w["wout"], w["fg"], w["wr"], w["br"], tm=min(512, S))
    R = 256
    cnt, off, total, lo, hi, nv, blk, n_tiles = _routing_tables(cnt[0], T, R)
    bucket, rank = meta[:, 0], meta[:, 1]
    xs = _dispatch_call(bucket, rank, off, cnt, total, xw, n_tiles, R, tb=min(2048, T))
    ys = _expert_call(lo, hi, nv, blk, xs, w["fg"], w["wg"], w["wu"], w["wd"], w["fin"], R)
    y = _return_call(bucket, rank, off, ys, d, tb=min(1024, T))
    return y.reshape(B, S, d)


def kernel(x_prompt, x_sample, mix_norm, w_in, q_a_norm, w_q_b, kv_a_norm, w_kv_b, w_o_attn, conv_w, conv_b, lru_a_w, lru_a_b, lru_x_w, lru_x_b, lru_lambda, w_o_lru, gate_b, w_out, ffn_norm, router_group_w, router_group_b, router_expert_w, router_expert_b, w_gate_e, w_up_e, w_down_e, final_norm):
    depth = mix_norm.shape[0]
    layers = []
    for l in range(depth):
        layers.append(_prep_params(dict(
            mix_norm=mix_norm[l], w_in=w_in[l], q_a_norm=q_a_norm[l], w_q_b=w_q_b[l],
            kv_a_norm=kv_a_norm[l], w_kv_b=w_kv_b[l], w_o_attn=w_o_attn[l], conv_w=conv_w[l],
            conv_b=conv_b[l], lru_a_w=lru_a_w[l], lru_a_b=lru_a_b[l], lru_x_w=lru_x_w[l],
            lru_x_b=lru_x_b[l], lru_lambda=lru_lambda[l], w_o_lru=w_o_lru[l], gate_b=gate_b[l],
            w_out=w_out[l], ffn_norm=ffn_norm[l], router_group_w=router_group_w[l],
            router_group_b=router_group_b[l], router_expert_w=router_expert_w[l],
            router_expert_b=router_expert_b[l], w_gate_e=w_gate_e[l], w_up_e=w_up_e[l],
            w_down_e=w_down_e[l], final_norm=final_norm)))
    assert depth == 1

    def trunk(x):
        for w in layers:
            x = _trunk(x, w)
        return x

    return (trunk(x_prompt), trunk(x_sample))
```

```python
import functools
import math

import jax
import jax.numpy as jnp
from jax import lax
from jax.experimental import pallas as pl
from jax.experimental.pallas import tpu as pltpu

N_HEADS = 8
QK_NOPE = 128
QK_ROPE = 64
V_HEAD = 128
Q_LORA = 384
KV_LORA = 256
ROPE_THETA = 10000.0
LRU_BLOCKS = 8
LRU_C = 8.0
N_GROUPS = 4
EXPERTS_PER_GROUP = 8
N_EXPERTS = N_GROUPS * EXPERTS_PER_GROUP
EPS = 1e-6

LANES = 128
HEAD_PAD = 256
LAT_COLS = Q_LORA + KV_LORA + LANES
PAIR_SLOTS = EXPERTS_PER_GROUP * EXPERTS_PER_GROUP
N_BUCKETS = N_GROUPS * PAIR_SLOTS
ROW_X = 8
ROW_XW = 16
DMA_GROUP = 16
SQRT_FLOOR = 1e-36
VMEM_LIMIT = 56 * 1024 * 1024

F32 = jnp.float32
BF16 = jnp.bfloat16


def _sigmoid(z):
    return 1.0 / (1.0 + jnp.exp2(z * (-math.log2(math.e))))


def _rms(x, g):
    var = jnp.mean(x * x, axis=-1, keepdims=True)
    return x * lax.rsqrt(var + EPS) * g


def _rope(pe, cos_t, nsin_lo, sin_hi):
    x2_to_lo = pltpu.roll(pe, 96, axis=1)
    x1_to_hi = pltpu.roll(pe, 32, axis=1)
    return pe * cos_t + x2_to_lo * nsin_lo + x1_to_hi * sin_hi


def _proj_kernel(x_ref, cos_ref, nsl_ref, sh_ref, mixg_ref, win_ref, qg_ref, wqb_ref, kvg_ref,
                 wkvb_ref, gateb_ref, q_ref, k_ref, v_ref, xlru_ref, gelu_ref, gatt_ref,
                 grec_ref, *, scale):
    d = x_ref.shape[-1]
    xn = _rms(x_ref[...], mixg_ref[...]).astype(BF16)
    lat = jnp.dot(xn, win_ref[:, 0:LAT_COLS], preferred_element_type=F32)
    cos_t, nsl, sh = cos_ref[...], nsl_ref[...], sh_ref[...]

    qn = _rms(lat[:, 0:Q_LORA], qg_ref[...]).astype(BF16)
    qf = jnp.dot(qn, wqb_ref[...], preferred_element_type=F32)
    for h in range(N_HEADS):
        base = h * HEAD_PAD
        q_ref[0, h, :, 0:LANES] = (qf[:, base:base + LANES] * scale).astype(BF16)
        pe = _rope(qf[:, base + LANES:base + HEAD_PAD], cos_t, nsl, sh)
        q_ref[0, h, :, LANES:HEAD_PAD] = (pe * scale).astype(BF16)

    kvn = _rms(lat[:, Q_LORA:Q_LORA + KV_LORA], kvg_ref[...]).astype(BF16)
    kvf = jnp.dot(kvn, wkvb_ref[...], preferred_element_type=F32)
    kpe = _rope(lat[:, Q_LORA + KV_LORA:LAT_COLS], cos_t, nsl, sh).astype(BF16)
    for h in range(N_HEADS):
        k_ref[0, h, :, 0:LANES] = kvf[:, h * LANES:(h + 1) * LANES].astype(BF16)
        k_ref[0, h, :, LANES:HEAD_PAD] = kpe
        v_ref[0, h, :, :] = kvf[:, (N_HEADS + h) * LANES:(N_HEADS + h + 1) * LANES].astype(BF16)

    c0 = LAT_COLS
    xlru_ref[...] = jnp.dot(xn, win_ref[:, c0:c0 + d], preferred_element_type=F32)
    g = jnp.dot(xn, win_ref[:, c0 + d:c0 + 2 * d], preferred_element_type=F32)
    gelu_ref[...] = jax.nn.gelu(g).astype(BF16)
    ga = jnp.dot(xn, win_ref[:, c0 + 2 * d:c0 + 3 * d], preferred_element_type=F32)
    gatt_ref[...] = _sigmoid(ga + gateb_ref[0:1, :]).astype(BF16)
    gb = jnp.dot(xn, win_ref[:, c0 + 3 * d:c0 + 4 * d], preferred_element_type=F32)
    grec_ref[...] = _sigmoid(gb + gateb_ref[1:2, :]).astype(BF16)


def _proj_call(x2d, B, S, tabs, mixg, win, qg, wqb, kvg, wkvb, gateb, tm):
    T, d = x2d.shape
    nt_seq = S // tm
    full = lambda a: pl.BlockSpec(a.shape, lambda i: (0,) * a.ndim)
    tab_spec = pl.BlockSpec((tm, LANES), lambda i: (i % nt_seq, 0))
    tok = lambda w: pl.BlockSpec((tm, w), lambda i: (i, 0))
    head_spec = lambda w: pl.BlockSpec((1, N_HEADS, tm, w), lambda i: (i // nt_seq, 0, i % nt_seq, 0))
    scale = math.log2(math.e) / math.sqrt(QK_NOPE + QK_ROPE)
    return pl.pallas_call(
        functools.partial(_proj_kernel, scale=scale),
        grid=(T // tm,),
        in_specs=[tok(d), tab_spec, tab_spec, tab_spec, full(mixg), full(win), full(qg), full(wqb),
                  full(kvg), full(wkvb), full(gateb)],
        out_specs=[head_spec(HEAD_PAD), head_spec(HEAD_PAD), head_spec(V_HEAD),
                   tok(d), tok(d), tok(d), tok(d)],
        out_shape=[
            jax.ShapeDtypeStruct((B, N_HEADS, S, HEAD_PAD), BF16),
            jax.ShapeDtypeStruct((B, N_HEADS, S, HEAD_PAD), BF16),
            jax.ShapeDtypeStruct((B, N_HEADS, S, V_HEAD), BF16),
            jax.ShapeDtypeStruct((T, d), F32),
            jax.ShapeDtypeStruct((T, d), BF16),
            jax.ShapeDtypeStruct((T, d), BF16),
            jax.ShapeDtypeStruct((T, d), BF16),
        ],
        compiler_params=pltpu.CompilerParams(dimension_semantics=("arbitrary",),
                                             vmem_limit_bytes=VMEM_LIMIT),
        name="proj",
    )(x2d, *tabs, mixg, win, qg, wqb, kvg, wkvb, gateb)


def _attn_kernel(q_ref, k_ref, v_ref, o_ref, s_ref, m_ref, l_ref, acc_ref, *, tk, unroll):
    q = q_ref[0, 0]
    nk = k_ref.shape[2] // tk
    n_rep = tk // LANES

    def scores(j):
        off = j * tk if isinstance(j, int) else pl.multiple_of(j * tk, tk)
        kc = k_ref[0, 0, pl.ds(off, tk), :]
        return lax.dot_general(q, kc, (((1,), (1,)), ((), ())), preferred_element_type=F32)

    def consume(j, s):
        off = j * tk if isinstance(j, int) else pl.multiple_of(j * tk, tk)
        m_old = m_ref[...]
        m_new = jnp.maximum(m_old, jnp.max(s, axis=-1, keepdims=True))
        alpha = jnp.exp2(m_old - m_new)
        p = jnp.exp2(s - jnp.tile(m_new, (1, n_rep)))
        psum = p[:, 0:LANES]
        for c in range(1, n_rep):
            psum = psum + p[:, c * LANES:(c + 1) * LANES]
        l_ref[...] = alpha * l_ref[...] + psum
        pv = jnp.dot(p.astype(BF16), v_ref[0, 0, pl.ds(off, tk), :], preferred_element_type=F32)
        acc_ref[...] = alpha * acc_ref[...] + pv
        m_ref[...] = m_new

    m_ref[...] = jnp.full(m_ref.shape, -jnp.inf, F32)
    l_ref[...] = jnp.zeros(l_ref.shape, F32)
    acc_ref[...] = jnp.zeros(acc_ref.shape, F32)
    s_ref[0] = scores(0)

    def step(j, cur):
        s = s_ref[cur]
        s_ref[1 - cur] = scores(j + 1)
        consume(j, s)

    if nk <= unroll:
        for j in range(nk - 1):
            step(j, j % 2)
    else:
        assert unroll % 2 == 0 and nk % unroll == 0
        def group(g, _):
            for u in range(unroll):
                step(unroll * g + u, u % 2)
            return 0
        lax.fori_loop(0, nk // unroll - 1, group, 0)
        for j in range(nk - unroll, nk - 1):
            step(j, j % 2)
    consume(nk - 1, s_ref[(nk - 1) % 2])
    l = jnp.sum(l_ref[...], axis=-1, keepdims=True)
    o_ref[0] = (acc_ref[...] / l).astype(BF16)


def _attn_call(q, k, v, tq, tk, unroll=8):
    B, H, S, _ = q.shape
    return pl.pallas_call(
        functools.partial(_attn_kernel, tk=tk, unroll=unroll),
        grid=(B, H, S // tq),
        in_specs=[
            pl.BlockSpec((1, 1, tq, HEAD_PAD), lambda b, h, i: (b, h, i, 0)),
            pl.BlockSpec((1, 1, S, HEAD_PAD), lambda b, h, i: (b, h, 0, 0)),
            pl.BlockSpec((1, 1, S, V_HEAD), lambda b, h, i: (b, h, 0, 0)),
        ],
        out_specs=pl.BlockSpec((1, tq, V_HEAD), lambda b, h, i: (b, i, h)),
        out_shape=jax.ShapeDtypeStruct((B, S, H * V_HEAD), BF16),
        scratch_shapes=[pltpu.VMEM((2, tq, tk), F32), pltpu.VMEM((tq, LANES), F32),
                        pltpu.VMEM((tq, LANES), F32), pltpu.VMEM((tq, V_HEAD), F32)],
        compiler_params=pltpu.CompilerParams(
            dimension_semantics=("arbitrary", "arbitrary", "arbitrary"),
            vmem_limit_bytes=VMEM_LIMIT),
        name="attn",
    )(q, k, v)


def _lru_kernel(*refs, reverse, n_chunks):
    if reverse:
        (x_ref, xp_ref, xn_ref, cw_ref, cb_ref, w_ref, ab_ref, xb_ref, lam_ref,
         out_ref, h_ref, a_s, b_s, h_s) = refs
    else:
        (x_ref, xp_ref, xn_ref, cw_ref, cb_ref, w_ref, ab_ref, xb_ref, lam_ref, hb_ref, gelu_ref,
         out_ref, h_ref, a_s, b_s, h_s) = refs
    j = pl.program_id(1)
    c = (n_chunks - 1 - j) if reverse else j
    L, W = x_ref.shape[1], x_ref.shape[2]
    blk = W // LRU_BLOCKS

    @pl.when(j == 0)
    def _():
        h_ref[...] = jnp.zeros(h_ref.shape, F32)

    x = x_ref[0]
    prev = jnp.where(c > 0, xp_ref[0], 0.0)
    nxt = jnp.where(c < n_chunks - 1, xn_ref[0], 0.0)
    row8 = lax.broadcasted_iota(jnp.int32, (8, W), 0)

    def shifted(k, head, tail):
        r = pltpu.roll(x, k % L, axis=0)
        return jnp.concatenate([head(r[0:8]), r[8:L - 8], tail(r[L - 8:L])], axis=0)

    keep = lambda s: s
    xm1 = shifted(1, lambda s: jnp.where(row8 == 0, prev[7:8, :], s), keep)
    xm2 = shifted(2, lambda s: jnp.where(row8 == 0, prev[6:7, :],
                                         jnp.where(row8 == 1, prev[7:8, :], s)), keep)
    xp1 = shifted(-1, keep, lambda s: jnp.where(row8 == 7, nxt[0:1, :], s))
    xc = cb_ref[...] + (xm2 * cw_ref[0:1, :] + xm1 * cw_ref[1:2, :] + x * cw_ref[2:3, :]
                        + xp1 * cw_ref[3:4, :])
    xcb = xc.astype(BF16)
    sp = jnp.log1p(jnp.exp(-jnp.abs(lam_ref[...]))) + jnp.maximum(-lam_ref[...], 0.0)
    sp2 = sp * (-LRU_C * math.log2(math.e))
    for n in range(LRU_BLOCKS):
        sl = slice(n * blk, (n + 1) * blk)
        z = jnp.dot(xcb[:, sl], w_ref[n], preferred_element_type=F32)
        r = _sigmoid(z[:, 0:blk] + ab_ref[:, sl])
        i = _sigmoid(z[:, blk:2 * blk] + xb_ref[:, sl])
        a = jnp.exp2(r * sp2[:, sl])
        a_s[:, sl] = a
        om = 1.0 - a * a
        root = om * lax.rsqrt(jnp.maximum(om, SQRT_FLOOR))
        b_s[:, sl] = root * (i * xc[:, sl])

    def step(t, h):
        tt = (L - 1 - t) if reverse else t
        h = a_s[pl.ds(tt, 1), :] * h + b_s[pl.ds(tt, 1), :]
        h_s[pl.ds(tt, 1), :] = h
        return h

    h_ref[...] = lax.fori_loop(0, L, step, h_ref[...], unroll=8)
    if reverse:
        out_ref[0] = h_s[...].astype(BF16)
    else:
        hsum = h_s[...] + hb_ref[0].astype(F32)
        out_ref[0] = (hsum * gelu_ref[0].astype(F32)).astype(BF16)


def _lru_call(x_lru, cw, cb, w, ab, xb, lam, hb, gelu, L, reverse):
    B, S, W = x_lru.shape
    nC = S // L
    pos = (lambda j: nC - 1 - j) if reverse else (lambda j: j)
    r8 = L // 8
    full = lambda a: pl.BlockSpec(a.shape, lambda b, j: (0,) * a.ndim)
    chunk = pl.BlockSpec((1, L, W), lambda b, j: (b, pos(j), 0))
    in_specs = [
        chunk,
        pl.BlockSpec((1, 8, W), lambda b, j: (b, jnp.maximum(pos(j) * r8 - 1, 0), 0)),
        pl.BlockSpec((1, 8, W), lambda b, j: (b, jnp.minimum((pos(j) + 1) * r8, S // 8 - 1), 0)),
        full(cw), full(cb), full(w), full(ab), full(xb), full(lam),
    ]
    args = [x_lru, x_lru, x_lru, cw, cb, w, ab, xb, lam]
    if not reverse:
        in_specs += [chunk, chunk]
        args += [hb, gelu]
    return pl.pallas_call(
        functools.partial(_lru_kernel, reverse=reverse, n_chunks=nC),
        grid=(B, nC),
        in_specs=in_specs,
        out_specs=chunk,
        out_shape=jax.ShapeDtypeStruct((B, S, W), BF16),
        scratch_shapes=[pltpu.VMEM((1, W), F32), pltpu.VMEM((L, W), F32), pltpu.VMEM((L, W), F32),
                        pltpu.VMEM((L, W), F32)],
        compiler_params=pltpu.CompilerParams(dimension_semantics=("arbitrary", "arbitrary"),
                                             vmem_limit_bytes=VMEM_LIMIT),
        name="lru_bwd" if reverse else "lru_fwd",
    )(*args)


def _merge_kernel(x_ref, o_ref, yl_ref, gatt_ref, grec_ref, woa_ref, wol_ref, wout_ref, fg_ref,
                  wr_ref, br_ref, xw_ref, meta_ref, cnt_ref):
    tm = x_ref.shape[0]
    y_attn = jnp.dot(o_ref[...], woa_ref[...], preferred_element_type=F32)
    y_lru = jnp.dot(yl_ref[...], wol_ref[...], preferred_element_type=F32)
    mix = gatt_ref[...].astype(F32) * y_attn + grec_ref[...].astype(F32) * y_lru
    x1 = x_ref[...] + jnp.dot(mix.astype(BF16), wout_ref[...], preferred_element_type=F32)
    xn2 = _rms(x1, fg_ref[...])

    x_hi = xn2.astype(BF16)
    x_lo = (xn2 - x_hi.astype(F32)).astype(BF16)
    hh = jnp.dot(x_hi, wr_ref[...], preferred_element_type=F32)
    lh = jnp.dot(x_lo, wr_ref[:, 0:LANES], preferred_element_type=F32)
    logits = hh[:, 0:LANES] + (hh[:, LANES:2 * LANES] + lh) + br_ref[...]
    lane = lax.broadcasted_iota(jnp.int32, logits.shape, 1)
    big = jnp.int32(LANES)
    neg = jnp.float32(-jnp.inf)
    is_g = (lane >= N_EXPERTS) & (lane < N_EXPERTS + N_GROUPS)
    gl = jnp.where(is_g, logits, neg)
    gmax = jnp.max(gl, axis=-1, keepdims=True)
    g_p = 1.0 / jnp.sum(jnp.exp(gl - gmax), axis=-1, keepdims=True)
    g_idx = jnp.min(jnp.where(gl == gmax, lane - N_EXPERTS, big), axis=-1, keepdims=True)
    in_grp = (lane < N_EXPERTS) & ((lane // EXPERTS_PER_GROUP) == g_idx)
    el = jnp.where(in_grp, logits, neg)
    m1 = jnp.max(el, axis=-1, keepdims=True)
    i1 = jnp.min(jnp.where(el == m1, lane, big), axis=-1, keepdims=True)
    el2 = jnp.where(lane == i1, neg, el)
    m2 = jnp.max(el2, axis=-1, keepdims=True)
    i2 = jnp.min(jnp.where(el2 == m2, lane, big), axis=-1, keepdims=True)
    e2 = jnp.exp(m2 - m1)
    w1 = g_p / (1.0 + e2)
    w2 = w1 * e2

    first_lo = i1 < i2
    e_lo = jnp.where(first_lo, i1, i2)
    e_hi = jnp.where(first_lo, i2, i1)
    w_lo = jnp.where(first_lo, w1, w2)
    w_hi = jnp.where(first_lo, w2, w1)
    bucket = g_idx * PAIR_SLOTS + (e_lo % EXPERTS_PER_GROUP) * EXPERTS_PER_GROUP + e_hi % EXPERTS_PER_GROUP

    @pl.when(pl.program_id(0) == 0)
    def _():
        cnt_ref[...] = jnp.zeros(cnt_ref.shape, F32)

    blane = lax.broadcasted_iota(jnp.int32, (tm, N_BUCKETS), 1)
    onehot = blane == bucket
    tri = (lax.broadcasted_iota(jnp.int32, (tm, tm), 0)
           > lax.broadcasted_iota(jnp.int32, (tm, tm), 1))
    earlier = jnp.dot(tri.astype(BF16), onehot.astype(BF16), preferred_element_type=F32)
    base = cnt_ref[0:1, :]
    rank = jnp.sum(jnp.where(onehot, earlier + base, 0.0), axis=-1, keepdims=True)
    cnt_ref[0:1, :] = base + jnp.sum(onehot.astype(F32), axis=0, keepdims=True)
    meta_ref[...] = jnp.where(lane == 0, bucket, jnp.where(lane == 1, rank.astype(jnp.int32), 0))

    for c in range(ROW_X):
        xw_ref[pl.ds(c, tm, stride=ROW_XW), :] = x1[:, c * LANES:(c + 1) * LANES]
    xw_ref[pl.ds(ROW_X, tm, stride=ROW_XW), :] = jnp.where(lane == 0, w_lo,
                                                        jnp.where(lane == 1, w_hi, 0.0))
    zeros = jnp.zeros((tm, LANES), F32)
    for c in range(ROW_X + 1, ROW_XW):
        xw_ref[pl.ds(c, tm, stride=ROW_XW), :] = zeros


def _merge_call(x2d, o, yl, gatt, grec, woa, wol, wout, fg, wr, br, tm):
    T, d = x2d.shape
    assert d == ROW_X * LANES
    full = lambda a: pl.BlockSpec(a.shape, lambda i: (0,) * a.ndim)
    tok = lambda w: pl.BlockSpec((tm, w), lambda i: (i, 0))
    return pl.pallas_call(
        _merge_kernel,
        grid=(T // tm,),
        in_specs=[tok(d), tok(d), tok(d), tok(d), tok(d), full(woa), full(wol), full(wout),
                  full(fg), full(wr), full(br)],
        out_specs=[pl.BlockSpec((tm * ROW_XW, LANES), lambda i: (i, 0)), tok(LANES),
                   pl.BlockSpec((8, N_BUCKETS), lambda i: (0, 0))],
        out_shape=[jax.ShapeDtypeStruct((T * ROW_XW, LANES), F32),
                   jax.ShapeDtypeStruct((T, LANES), jnp.int32),
                   jax.ShapeDtypeStruct((8, N_BUCKETS), F32)],
        compiler_params=pltpu.CompilerParams(dimension_semantics=("arbitrary",),
                                             vmem_limit_bytes=VMEM_LIMIT),
        name="merge",
    )(x2d, o, yl, gatt, grec, woa, wol, wout, fg, wr, br)


def _row_copy(src, src_row, dst, dst_row, rows, sem):
    return pltpu.make_async_copy(src.at[pl.ds(src_row * rows, rows)],
                                 dst.at[pl.ds(dst_row * rows, rows)], sem)


def _for_row_groups(n_rows, slot_of, copy_of):
    def start(g, _):
        rows = [g * DMA_GROUP + u for u in range(DMA_GROUP)]
        slots = [slot_of(t) for t in rows]
        for t, s in zip(rows, slots):
            copy_of(t, s).start()
        return 0

    def wait(g, _):
        for u in range(DMA_GROUP):
            copy_of(g * DMA_GROUP + u, 0).wait()
        return 0

    lax.fori_loop(0, n_rows // DMA_GROUP, start, 0)
    lax.fori_loop(0, n_rows // DMA_GROUP, wait, 0)


def _fill_unused_slots(cnt_ref, off_ref, total_ref, zero_ref, xs_ref, sem, R, n_tiles):
    zero_ref[...] = jnp.zeros(zero_ref.shape, F32)
    bits = R.bit_length() - 1

    def run_copy(pos, size):
        return pltpu.make_async_copy(zero_ref.at[pl.ds(0, size * ROW_XW)],
                                     xs_ref.at[pl.ds(pos * ROW_XW, size * ROW_XW)], sem)

    def bucket_pads(b, op):
        pad = (-cnt_ref[b]) & (R - 1)
        pos = off_ref[b] + cnt_ref[b]
        for k in range(bits):
            has = (pad >> k) & 1

            @pl.when(has == 1)
            def _():
                op(run_copy(pos, 1 << k))

            pos = pos + has * (1 << k)

    def tail_tile(i, op):
        op(run_copy(i * R, R))

    for op in (lambda c: c.start(), lambda c: c.wait()):
        lax.fori_loop(0, N_BUCKETS, lambda b, _: (bucket_pads(b, op), 0)[1], 0)
        lax.fori_loop(total_ref[0], n_tiles, lambda i, _: (tail_tile(i, op), 0)[1], 0)


def _dispatch_kernel(bucket_ref, rank_ref, off_ref, cnt_ref, total_ref, xw_ref, xs_ref, zero_ref,
                     sem, fill_sem, *, tb, R, n_tiles):
    _for_row_groups(
        tb,
        lambda t: off_ref[bucket_ref[t]] + rank_ref[t],
        lambda t, slot: _row_copy(xw_ref, t, xs_ref, slot, ROW_XW, sem))

    @pl.when(pl.program_id(0) == 0)
    def _():
        _fill_unused_slots(cnt_ref, off_ref, total_ref, zero_ref, xs_ref, fill_sem, R, n_tiles)


def _dispatch_call(bucket, rank, off, cnt, total, xw, n_tiles, R, tb):
    T = bucket.shape[0]
    smem_blk = pl.BlockSpec((tb,), lambda i: (i,), memory_space=pltpu.SMEM)
    smem = pl.BlockSpec(memory_space=pltpu.SMEM)
    return pl.pallas_call(
        functools.partial(_dispatch_kernel, tb=tb, R=R, n_tiles=n_tiles),
        grid=(T // tb,),
        in_specs=[smem_blk, smem_blk, smem, smem, smem,
                  pl.BlockSpec((tb * ROW_XW, LANES), lambda i: (i, 0))],
        out_specs=pl.BlockSpec(memory_space=pl.ANY),
        out_shape=jax.ShapeDtypeStruct((n_tiles * R * ROW_XW, LANES), F32),
        scratch_shapes=[pltpu.VMEM((R * ROW_XW, LANES), F32), pltpu.SemaphoreType.DMA(()),
                        pltpu.SemaphoreType.DMA(())],
        compiler_params=pltpu.CompilerParams(dimension_semantics=("arbitrary",),
                                             vmem_limit_bytes=VMEM_LIMIT),
        name="dispatch",
    )(bucket, rank, off, cnt, total, xw)


def _expert_kernel(lo_ref, hi_ref, nv_ref, blk_ref, xs_ref, fg_ref, wgl_ref, wul_ref, wdl_ref,
                   wgh_ref, wuh_ref, wdh_ref, fin_ref, ys_ref, *, R):
    nv = nv_ref[pl.program_id(0)]

    @pl.when(nv == 0)
    def _():
        ys_ref[...] = jnp.zeros(ys_ref.shape, F32)

    @pl.when(nv > 0)
    def _():
        x1 = jnp.concatenate([xs_ref[pl.ds(c, R, stride=ROW_XW), :] for c in range(ROW_X)], axis=1)
        wrow = xs_ref[pl.ds(ROW_X, R, stride=ROW_XW), :]
        lane = lax.broadcasted_iota(jnp.int32, wrow.shape, 1)
        w_lo = jnp.sum(jnp.where(lane == 0, wrow, 0.0), axis=-1, keepdims=True)
        w_hi = jnp.sum(jnp.where(lane == 1, wrow, 0.0), axis=-1, keepdims=True)
        xn = _rms(x1, fg_ref[...]).astype(BF16)

        def hidden(wg_ref, wu_ref, w):
            gt = jnp.dot(xn, wg_ref[0], preferred_element_type=F32)
            up = jnp.dot(xn, wu_ref[0], preferred_element_type=F32)
            return ((gt * _sigmoid(gt)) * up * w).astype(BF16)

        moe = (jnp.dot(hidden(wgl_ref, wul_ref, w_lo), wdl_ref[0], preferred_element_type=F32)
               + jnp.dot(hidden(wgh_ref, wuh_ref, w_hi), wdh_ref[0], preferred_element_type=F32))
        y = _rms(x1 + moe, fin_ref[...])
        for c in range(ROW_X):
            ys_ref[pl.ds(c, R, stride=ROW_X), :] = y[:, c * LANES:(c + 1) * LANES]


def _expert_call(tile_lo, tile_hi, tile_nv, tile_blk, xs, fg, wg, wu, wd, fin, R):
    n_tiles = tile_lo.shape[0]
    E, d, de = wg.shape
    lo_map = lambda i, lo, hi, nv, blk: (lo[i], 0, 0)
    hi_map = lambda i, lo, hi, nv, blk: (hi[i], 0, 0)
    const = lambda a: pl.BlockSpec(a.shape, lambda i, lo, hi, nv, blk: (0,) * a.ndim)
    grid_spec = pltpu.PrefetchScalarGridSpec(
        num_scalar_prefetch=4,
        grid=(n_tiles,),
        in_specs=[pl.BlockSpec((R * ROW_XW, LANES), lambda i, lo, hi, nv, blk: (blk[i], 0)),
                  const(fg),
                  pl.BlockSpec((1, d, de), lo_map), pl.BlockSpec((1, d, de), lo_map),
                  pl.BlockSpec((1, de, d), lo_map),
                  pl.BlockSpec((1, d, de), hi_map), pl.BlockSpec((1, d, de), hi_map),
                  pl.BlockSpec((1, de, d), hi_map),
                  const(fin)],
        out_specs=pl.BlockSpec((R * ROW_X, LANES), lambda i, lo, hi, nv, blk: (i, 0)),
    )
    return pl.pallas_call(
        functools.partial(_expert_kernel, R=R),
        grid_spec=grid_spec,
        out_shape=jax.ShapeDtypeStruct((n_tiles * R * ROW_X, LANES), F32),
        compiler_params=pltpu.CompilerParams(dimension_semantics=("arbitrary",),
                                             vmem_limit_bytes=VMEM_LIMIT),
        name="experts",
    )(tile_lo, tile_hi, tile_nv, tile_blk, xs, fg, wg, wu, wd, wg, wu, wd, fin)


def _return_kernel(bucket_ref, rank_ref, off_ref, ys_ref, y_ref, buf_ref, sem, *, tb):
    _for_row_groups(
        tb,
        lambda t: off_ref[bucket_ref[t]] + rank_ref[t],
        lambda t, slot: _row_copy(ys_ref, slot, buf_ref, t, ROW_X, sem))
    for c in range(ROW_X):
        y_ref[:, c * LANES:(c + 1) * LANES] = buf_ref[pl.ds(c, tb, stride=ROW_X), :]


def _return_call(bucket, rank, off, ys, d, tb):
    T = bucket.shape[0]
    smem_blk = pl.BlockSpec((tb,), lambda i: (i,), memory_space=pltpu.SMEM)
    return pl.pallas_call(
        functools.partial(_return_kernel, tb=tb),
        grid=(T // tb,),
        in_specs=[smem_blk, smem_blk, pl.BlockSpec(memory_space=pltpu.SMEM),
                  pl.BlockSpec(memory_space=pl.ANY)],
        out_specs=pl.BlockSpec((tb, d), lambda i: (i, 0)),
        out_shape=jax.ShapeDtypeStruct((T, d), F32),
        scratch_shapes=[pltpu.VMEM((tb * ROW_X, LANES), F32), pltpu.SemaphoreType.DMA(())],
        compiler_params=pltpu.CompilerParams(dimension_semantics=("arbitrary",),
                                             vmem_limit_bytes=VMEM_LIMIT),
        name="moe_return",
    )(bucket, rank, off, ys)


def _routing_tables(counts, T, R):
    n_tiles = T // R + N_GROUPS * (EXPERTS_PER_GROUP * (EXPERTS_PER_GROUP - 1) // 2)
    cnt = counts.astype(jnp.int32)
    tiles_b = (cnt + R - 1) // R
    tile_end = jnp.cumsum(tiles_b)
    tile_start = tile_end - tiles_b
    off = tile_start * R
    total = tile_end[-1]
    i = jnp.arange(n_tiles, dtype=jnp.int32)
    blk = jnp.minimum(i, total - 1)
    b = jnp.searchsorted(tile_end, blk, side="right").astype(jnp.int32)
    nv = jnp.where(i < total, jnp.clip(cnt[b] - (blk - tile_start[b]) * R, 0, R), 0)
    grp = (b // PAIR_SLOTS) * EXPERTS_PER_GROUP
    lo = grp + (b % PAIR_SLOTS) // EXPERTS_PER_GROUP
    hi = grp + b % EXPERTS_PER_GROUP
    return cnt, off, total.reshape(1), lo, hi, nv.astype(jnp.int32), blk, n_tiles


def _rope_tables(S):
    inv = ROPE_THETA ** (-jnp.arange(0, QK_ROPE, 2, dtype=F32) / QK_ROPE)
    ang = jnp.arange(S, dtype=F32)[:, None] * inv[None, :]
    cos, sin = jnp.cos(ang), jnp.sin(ang)
    half = QK_ROPE // 2
    z = lambda n: jnp.zeros((S, n), F32)
    cos_t = jnp.concatenate([cos, cos, z(LANES - QK_ROPE)], axis=1)
    nsin_lo = jnp.concatenate([-sin, z(LANES - half)], axis=1)
    sin_hi = jnp.concatenate([z(half), sin, z(LANES - QK_ROPE)], axis=1)
    return cos_t, nsin_lo, sin_hi


def _prep_params(p):
    d = p["w_in"].shape[0]
    row = lambda a: a.reshape(1, -1).astype(F32)
    w_in = p["w_in"]
    c_q, c_kv, c_r = Q_LORA, Q_LORA + KV_LORA, Q_LORA + KV_LORA + QK_ROPE
    win = jnp.concatenate(
        [w_in[:, :c_r], jnp.zeros((d, LANES - QK_ROPE), F32), w_in[:, c_r:]], axis=1).astype(BF16)
    wqb = p["w_q_b"].reshape(Q_LORA, N_HEADS, QK_NOPE + QK_ROPE)
    wqb = jnp.pad(wqb, ((0, 0), (0, 0), (0, HEAD_PAD - QK_NOPE - QK_ROPE)))
    wqb = wqb.reshape(Q_LORA, N_HEADS * HEAD_PAD).astype(BF16)
    wkvb = p["w_kv_b"].reshape(KV_LORA, N_HEADS, QK_NOPE + V_HEAD)
    wkvb = jnp.concatenate([wkvb[:, :, :QK_NOPE].reshape(KV_LORA, -1),
                            wkvb[:, :, QK_NOPE:].reshape(KV_LORA, -1)], axis=1).astype(BF16)
    wlru = jnp.concatenate([p["lru_a_w"], p["lru_x_w"]], axis=-1).astype(BF16)
    wr = jnp.concatenate([p["router_expert_w"], p["router_group_w"],
                          jnp.zeros((d, LANES - N_EXPERTS - N_GROUPS), F32)], axis=1)
    wr_hi = wr.astype(BF16)
    wr = jnp.concatenate([wr_hi, (wr - wr_hi.astype(F32)).astype(BF16)], axis=1)
    br =jnp.concatenate([p["router_expert_b"], p["router_group_b"],
                          jnp.zeros((LANES - N_EXPERTS - N_GROUPS,), F32)]).reshape(1, LANES)
    return dict(
        mixg=row(p["mix_norm"]), win=win, qg=row(p["q_a_norm"]), wqb=wqb, kvg=row(p["kv_a_norm"]),
        wkvb=wkvb, gateb=p["gate_b"].astype(F32), cw=p["conv_w"].astype(F32), cb=row(p["conv_b"]),
        wlru=wlru, ab=p["lru_a_b"].astype(F32), xb=p["lru_x_b"].astype(F32),
        lam=p["lru_lambda"].astype(F32), woa=p["w_o_attn"].astype(BF16),
        wol=p["w_o_lru"].astype(BF16), wout=p["w_out"].astype(BF16), fg=row(p["ffn_norm"]),
        wr=wr, br=br, wg=p["w_gate_e"].astype(BF16), wu=p["w_up_e"].astype(BF16),
        wd=p["w_down_e"].astype(BF16), fin=row(p["final_norm"]))


def _trunk(x, w):
    B, S, d = x.shape
    T = B * S
    x2d = x.reshape(T, d)
    tm = min(256, S)
    tabs = _rope_tables(S)
    q, k, v, x_lru, gelu_g, gatt, grec = _proj_call(
        x2d, B, S, tabs, w["mixg"], w["win"], w["qg"], w["wqb"], w["kvg"], w["wkvb"], w["gateb"], tm)
    o = _attn_call(q, k, v, tq=min(1024, S), tk=min(1024, S), unroll=4)

    L = min(256, S)
    x_lru3 = x_lru.reshape(B, S, d)
    gelu3 = gelu_g.reshape(B, S, d)
    hb = _lru_call(x_lru3, w["cw"], w["cb"], w["wlru"][1], w["ab"][1:2], w["xb"][1:2],
                   w["lam"][1:2], None, None, L, reverse=True)
    yl = _lru_call(x_lru3, w["cw"], w["cb"], w["wlru"][0], w["ab"][0:1], w["xb"][0:1],
                   w["lam"][0:1], hb, gelu3, L, reverse=False)

    xw, meta, cnt = _merge_call(x2d, o.reshape(T, d), yl.reshape(T, d), gatt, grec, w["woa"],
                                w["wol"], w["wout"], w["fg"], w["wr"], w["br"], tm=min(512, S))
    R = 256
    cnt, off, total, lo, hi, nv, blk, n_tiles = _routing_tables(cnt[0], T, R)
    bucket, rank = meta[:, 0], meta[:, 1]
    xs = _dispatch_call(bucket, rank, off, cnt, total, xw, n_tiles, R, tb=min(1024, T))
    ys = _expert_call(lo, hi, nv, blk, xs, w["fg"], w["wg"], w["wu"], w["wd"], w["fin"], R)
    y = _return_call(bucket, rank, off, ys, d, tb=min(1024, T))
    return y.reshape(B, S, d)


def kernel(x_prompt, x_sample, mix_norm, w_in, q_a_norm, w_q_b, kv_a_norm, w_kv_b, w_o_attn, conv_w, conv_b, lru_a_w, lru_a_b, lru_x_w, lru_x_b, lru_lambda, w_o_lru, gate_b, w_out, ffn_norm, router_group_w, router_group_b, router_expert_w, router_expert_b, w_gate_e, w_up_e, w_down_e, final_norm):
    depth = mix_norm.shape[0]
    layers = []
    for l in range(depth):
        layers.append(_prep_params(dict(
            mix_norm=mix_norm[l], w_in=w_in[l], q_a_norm=q_a_norm[l], w_q_b=w_q_b[l],
            kv_a_norm=kv_a_norm[l], w_kv_b=w_kv_b[l], w_o_attn=w_o_attn[l], conv_w=conv_w[l],
            conv_b=conv_b[l], lru_a_w=lru_a_w[l], lru_a_b=lru_a_b[l], lru_x_w=lru_x_w[l],
            lru_x_b=lru_x_b[l], lru_lambda=lru_lambda[l], w_o_lru=w_o_lru[l], gate_b=gate_b[l],
            w_out=w_out[l], ffn_norm=ffn_norm[l], router_group_w=router_group_w[l],
            router_group_b=router_group_b[l], router_expert_w=router_expert_w[l],
            router_expert_b=router_expert_b[l], w_gate_e=w_gate_e[l], w_up_e=w_up_e[l],
            w_down_e=w_down_e[l], final_norm=final_norm)))
    assert depth == 1

    def trunk(x):
        for w in layers:
            x = _trunk(x, w)
        return x

    return (trunk(x_prompt), trunk(x_sample))
```

```python
import functools
import math

import jax
import jax.numpy as jnp
from jax import lax
from jax.experimental import pallas as pl
from jax.experimental.pallas import tpu as pltpu

N_HEADS = 8
QK_NOPE = 128
QK_ROPE = 64
V_HEAD = 128
Q_LORA = 384
KV_LORA = 256
ROPE_THETA = 10000.0
LRU_BLOCKS = 8
LRU_C = 8.0
N_GROUPS = 4
EXPERTS_PER_GROUP = 8
N_EXPERTS = N_GROUPS * EXPERTS_PER_GROUP
EPS = 1e-6

LANES = 128
HEAD_PAD = 256
LAT_COLS = Q_LORA + KV_LORA + LANES
PAIR_SLOTS = EXPERTS_PER_GROUP * EXPERTS_PER_GROUP
N_BUCKETS = N_GROUPS * PAIR_SLOTS
ROW = 8
ROW_PK = 4
META_BUCKET, META_RANK, META_W_LO, META_W_HI = 0, 1, 2, 3
HI16 = -65536
DMA_GROUP = 16
SQRT_FLOOR = 1e-36
VMEM_LIMIT = 56 * 1024 * 1024

F32 = jnp.float32
BF16 = jnp.bfloat16


def _sigmoid(z):
    return 1.0 / (1.0 + jnp.exp2(z * (-math.log2(math.e))))


def _rms(x, g):
    var = jnp.mean(x * x, axis=-1, keepdims=True)
    return x * lax.rsqrt(var + EPS) * g


def _rope(pe, cos_t, nsin_lo, sin_hi):
    x2_to_lo = pltpu.roll(pe, 96, axis=1)
    x1_to_hi = pltpu.roll(pe, 32, axis=1)
    return pe * cos_t + x2_to_lo * nsin_lo + x1_to_hi * sin_hi


def _conv_time(x, prev, nxt, cw_ref, cb_ref):
    L, W = x.shape
    row8 = lax.broadcasted_iota(jnp.int32, (8, W), 0)

    def shifted(k, head, tail):
        r = pltpu.roll(x, k % L, axis=0)
        return jnp.concatenate([head(r[0:8]), r[8:L - 8], tail(r[L - 8:L])], axis=0)

    keep = lambda s: s
    xm1 = shifted(1, lambda s: jnp.where(row8 == 0, prev[7:8, :], s), keep)
    xm2 = shifted(2, lambda s: jnp.where(row8 == 0, prev[6:7, :],
                                         jnp.where(row8 == 1, prev[7:8, :], s)), keep)
    xp1 = shifted(-1, keep, lambda s: jnp.where(row8 == 7, nxt[0:1, :], s))
    return cb_ref[...] + (xm2 * cw_ref[0:1, :] + xm1 * cw_ref[1:2, :] + x * cw_ref[2:3, :]
                          + xp1 * cw_ref[3:4, :])


def _proj_kernel(x_ref, xp_ref, xn_ref, cos_ref, nsl_ref, sh_ref, mixg_ref, win_ref, qg_ref,
                 wqb_ref, kvg_ref, wkvb_ref, gateb_ref, cw_ref, cb_ref, q_ref, k_ref, v_ref,
                 xc_ref, gelu_ref, gatt_ref, grec_ref, *, scale, nt_seq):
    d = x_ref.shape[-1]
    xn = _rms(x_ref[...], mixg_ref[...]).astype(BF16)
    lat = jnp.dot(xn, win_ref[:, 0:LAT_COLS], preferred_element_type=F32)
    cos_t, nsl, sh = cos_ref[...], nsl_ref[...], sh_ref[...]

    qn = _rms(lat[:, 0:Q_LORA], qg_ref[...]).astype(BF16)
    qf = jnp.dot(qn, wqb_ref[...], preferred_element_type=F32)
    for h in range(N_HEADS):
        base = h * HEAD_PAD
        q_ref[0, h, :, 0:LANES] = (qf[:, base:base + LANES] * scale).astype(BF16)
        pe = _rope(qf[:, base + LANES:base + HEAD_PAD], cos_t, nsl, sh)
        q_ref[0, h, :, LANES:HEAD_PAD] = (pe * scale).astype(BF16)

    kvn = _rms(lat[:, Q_LORA:Q_LORA + KV_LORA], kvg_ref[...]).astype(BF16)
    kvf = jnp.dot(kvn, wkvb_ref[...], preferred_element_type=F32)
    kpe = _rope(lat[:, Q_LORA + KV_LORA:LAT_COLS], cos_t, nsl, sh).astype(BF16)
    for h in range(N_HEADS):
        k_ref[0, h, :, 0:LANES] = kvf[:, h * LANES:(h + 1) * LANES].astype(BF16)
        k_ref[0, h, :, LANES:HEAD_PAD] = kpe
        v_ref[0, h, :, :] = kvf[:, (N_HEADS + h) * LANES:(N_HEADS + h + 1) * LANES].astype(BF16)

    c0 = LAT_COLS
    ti = pl.program_id(0) % nt_seq
    halo = jnp.concatenate([xp_ref[...], xn_ref[...]], axis=0)
    halo = jnp.dot(_rms(halo, mixg_ref[...]).astype(BF16), win_ref[:, c0:c0 + d],
                   preferred_element_type=F32)
    prev = jnp.where(ti > 0, halo[0:8], 0.0)
    nxt = jnp.where(ti < nt_seq - 1, halo[8:16], 0.0)
    xl = jnp.dot(xn, win_ref[:, c0:c0 + d], preferred_element_type=F32)
    xc_ref[...] = _conv_time(xl, prev, nxt, cw_ref, cb_ref)
    g = jnp.dot(xn, win_ref[:, c0 + d:c0 + 2 * d], preferred_element_type=F32)
    gelu_ref[...] = jax.nn.gelu(g).astype(BF16)
    ga = jnp.dot(xn, win_ref[:, c0 + 2 * d:c0 + 3 * d], preferred_element_type=F32)
    gatt_ref[...] = _sigmoid(ga + gateb_ref[0:1, :]).astype(BF16)
    gb = jnp.dot(xn, win_ref[:, c0 + 3 * d:c0 + 4 * d], preferred_element_type=F32)
    grec_ref[...] = _sigmoid(gb + gateb_ref[1:2, :]).astype(BF16)


def _proj_call(x2d, B, S, tabs, mixg, win, qg, wqb, kvg, wkvb, gateb, cw, cb, tm):
    T, d = x2d.shape
    nt_seq = S // tm
    r8 = tm // 8
    halo_prev = pl.BlockSpec((8, d), lambda i: (jnp.maximum(i * r8 - 1, 0), 0))
    halo_next = pl.BlockSpec((8, d), lambda i: (jnp.minimum((i + 1) * r8, T // 8 - 1), 0))
    full = lambda a: pl.BlockSpec(a.shape, lambda i: (0,) * a.ndim, pipeline_mode=pl.Buffered(1))
    tab_spec = pl.BlockSpec((tm, LANES), lambda i: (i % nt_seq, 0))
    tok = lambda w: pl.BlockSpec((tm, w), lambda i: (i, 0))
    head_spec = lambda w: pl.BlockSpec((1, N_HEADS, tm, w), lambda i: (i // nt_seq, 0, i % nt_seq, 0))
    scale = math.log2(math.e) / math.sqrt(QK_NOPE + QK_ROPE)
    return pl.pallas_call(
        functools.partial(_proj_kernel, scale=scale, nt_seq=nt_seq),
        grid=(T // tm,),
        in_specs=[tok(d), halo_prev, halo_next, tab_spec, tab_spec, tab_spec, full(mixg), full(win),
                  full(qg), full(wqb), full(kvg), full(wkvb), full(gateb), full(cw), full(cb)],
        out_specs=[head_spec(HEAD_PAD), head_spec(HEAD_PAD), head_spec(V_HEAD),
                   tok(d), tok(d), tok(d), tok(d)],
        out_shape=[
            jax.ShapeDtypeStruct((B, N_HEADS, S, HEAD_PAD), BF16),
            jax.ShapeDtypeStruct((B, N_HEADS, S, HEAD_PAD), BF16),
            jax.ShapeDtypeStruct((B, N_HEADS, S, V_HEAD), BF16),
            jax.ShapeDtypeStruct((T, d), F32),
            jax.ShapeDtypeStruct((T, d), BF16),
            jax.ShapeDtypeStruct((T, d), BF16),
            jax.ShapeDtypeStruct((T, d), BF16),
        ],
        compiler_params=pltpu.CompilerParams(dimension_semantics=("arbitrary",),
                                             vmem_limit_bytes=VMEM_LIMIT),
        name="proj",
    )(x2d, x2d, x2d, *tabs, mixg, win, qg, wqb, kvg, wkvb, gateb, cw, cb)


def _attn_kernel(q_ref, k_ref, v_ref, o_ref, s_ref, m_ref, l_ref, acc_ref, *, tk, unroll):
    q = q_ref[0, 0]
    nk = k_ref.shape[2] // tk
    n_rep = tk // LANES

    def scores(j):
        off = j * tk if isinstance(j, int) else pl.multiple_of(j * tk, tk)
        kc = k_ref[0, 0, pl.ds(off, tk), :]
        return lax.dot_general(q, kc, (((1,), (1,)), ((), ())), preferred_element_type=F32)

    def consume(j, s):
        off = j * tk if isinstance(j, int) else pl.multiple_of(j * tk, tk)
        m_old = m_ref[...]
        m_new = jnp.maximum(m_old, jnp.max(s, axis=-1, keepdims=True))
        alpha = jnp.exp2(m_old - m_new)
        p = jnp.exp2(s - jnp.tile(m_new, (1, n_rep)))
        psum = p[:, 0:LANES]
        for c in range(1, n_rep):
            psum = psum + p[:, c * LANES:(c + 1) * LANES]
        l_ref[...] = alpha * l_ref[...] + psum
        pv = jnp.dot(p.astype(BF16), v_ref[0, 0, pl.ds(off, tk), :], preferred_element_type=F32)
        acc_ref[...] = alpha * acc_ref[...] + pv
        m_ref[...] = m_new

    m_ref[...] = jnp.full(m_ref.shape, -jnp.inf, F32)
    l_ref[...] = jnp.zeros(l_ref.shape, F32)
    acc_ref[...] = jnp.zeros(acc_ref.shape, F32)
    s_ref[0] = scores(0)

    def step(j, cur):
        s = s_ref[cur]
        s_ref[1 - cur] = scores(j + 1)
        consume(j, s)

    if nk <= unroll:
        for j in range(nk - 1):
            step(j, j % 2)
    else:
        assert unroll % 2 == 0 and nk % unroll == 0
        def group(g, _):
            for u in range(unroll):
                step(unroll * g + u, u % 2)
            return 0
        lax.fori_loop(0, nk // unroll - 1, group, 0)
        for j in range(nk - unroll, nk - 1):
            step(j, j % 2)
    consume(nk - 1, s_ref[(nk - 1) % 2])
    l = jnp.sum(l_ref[...], axis=-1, keepdims=True)
    o_ref[0] = (acc_ref[...] / l).astype(BF16)


def _attn_call(q, k, v, tq, tk, unroll=8):
    B, H, S, _ = q.shape
    return pl.pallas_call(
        functools.partial(_attn_kernel, tk=tk, unroll=unroll),
        grid=(B, H, S // tq),
        in_specs=[
            pl.BlockSpec((1, 1, tq, HEAD_PAD), lambda b, h, i: (b, h, i, 0)),
            pl.BlockSpec((1, 1, S, HEAD_PAD), lambda b, h, i: (b, h, 0, 0)),
            pl.BlockSpec((1, 1, S, V_HEAD), lambda b, h, i: (b, h, 0, 0)),
        ],
        out_specs=pl.BlockSpec((1, tq, V_HEAD), lambda b, h, i: (b, i, h)),
        out_shape=jax.ShapeDtypeStruct((B, S, H * V_HEAD), BF16),
        scratch_shapes=[pltpu.VMEM((2, tq, tk), F32), pltpu.VMEM((tq, LANES), F32),
                        pltpu.VMEM((tq, LANES), F32), pltpu.VMEM((tq, V_HEAD), F32)],
        compiler_params=pltpu.CompilerParams(
            dimension_semantics=("arbitrary", "arbitrary", "arbitrary"),
            vmem_limit_bytes=VMEM_LIMIT),
        name="attn",
    )(q, k, v)


def _lru_kernel(*refs, reverse):
    if reverse:
        (x_ref, w_ref, ab_ref, xb_ref, lam_ref, out_ref, h_ref, a_s, b_s, h_s) = refs
    else:
        (x_ref, w_ref, ab_ref, xb_ref, lam_ref, hb_ref, gelu_ref,
         out_ref, h_ref, a_s, b_s, h_s) = refs
    L, W = x_ref.shape[1], x_ref.shape[2]
    blk = W // LRU_BLOCKS

    @pl.when(pl.program_id(1) == 0)
    def _():
        h_ref[...] = jnp.zeros(h_ref.shape, F32)

    xc = x_ref[0]
    xcb = xc.astype(BF16)
    sp = jnp.log1p(jnp.exp(-jnp.abs(lam_ref[...]))) + jnp.maximum(-lam_ref[...], 0.0)
    sp2 = sp * (-LRU_C * math.log2(math.e))
    for n in range(LRU_BLOCKS):
        sl = slice(n * blk, (n + 1) * blk)
        z = jnp.dot(xcb[:, sl], w_ref[n], preferred_element_type=F32)
        r = _sigmoid(z[:, 0:blk] + ab_ref[:, sl])
        i = _sigmoid(z[:, blk:2 * blk] + xb_ref[:, sl])
        a = jnp.exp2(r * sp2[:, sl])
        a_s[:, sl] = a
        om = 1.0 - a * a
        root = om * lax.rsqrt(jnp.maximum(om, SQRT_FLOOR))
        b_s[:, sl] = root * (i * xc[:, sl])

    def step(t, h):
        tt = (L - 1 - t) if reverse else t
        h = a_s[pl.ds(tt, 1), :] * h + b_s[pl.ds(tt, 1), :]
        h_s[pl.ds(tt, 1), :] = h
        return h

    h_ref[...] = lax.fori_loop(0, L, step, h_ref[...], unroll=8)
    if reverse:
        out_ref[0] = h_s[...].astype(BF16)
    else:
        hsum = h_s[...] + hb_ref[0].astype(F32)
        out_ref[0] = (hsum * gelu_ref[0].astype(F32)).astype(BF16)


def _lru_call(xc, w, ab, xb, lam, hb, gelu, L, reverse):
    B, S, W = xc.shape
    nC = S // L
    pos = (lambda j: nC - 1 - j) if reverse else (lambda j: j)
    full = lambda a: pl.BlockSpec(a.shape, lambda b, j: (0,) * a.ndim)
    chunk = pl.BlockSpec((1, L, W), lambda b, j: (b, pos(j), 0))
    in_specs = [chunk, full(w), full(ab), full(xb), full(lam)]
    args = [xc, w, ab, xb, lam]
    if not reverse:
        in_specs += [chunk, chunk]
        args += [hb, gelu]
    return pl.pallas_call(
        functools.partial(_lru_kernel, reverse=reverse),
        grid=(B, nC),
        in_specs=in_specs,
        out_specs=chunk,
        out_shape=jax.ShapeDtypeStruct((B, S, W), BF16),
        scratch_shapes=[pltpu.VMEM((1, W), F32), pltpu.VMEM((L, W), F32), pltpu.VMEM((L, W), F32),
                        pltpu.VMEM((L, W), F32)],
        compiler_params=pltpu.CompilerParams(dimension_semantics=("arbitrary", "arbitrary"),
                                             vmem_limit_bytes=VMEM_LIMIT),
        name="lru_bwd" if reverse else "lru_fwd",
    )(*args)


def _merge_kernel(x_ref, o_ref, yl_ref, gatt_ref, grec_ref, woa_ref, wol_ref, wout_ref, fg_ref,
                  wr_ref, br_ref, x1_ref, pk_ref, meta_ref, cnt_ref):
    tm = x_ref.shape[0]
    y_attn = jnp.dot(o_ref[...], woa_ref[...], preferred_element_type=F32)
    y_lru = jnp.dot(yl_ref[...], wol_ref[...], preferred_element_type=F32)
    mix = gatt_ref[...].astype(F32) * y_attn + grec_ref[...].astype(F32) * y_lru
    x1 = x_ref[...] + jnp.dot(mix.astype(BF16), wout_ref[...], preferred_element_type=F32)
    xn2 = _rms(x1, fg_ref[...])

    x_hi = xn2.astype(BF16)
    x_lo = (xn2 - x_hi.astype(F32)).astype(BF16)
    hh = jnp.dot(x_hi, wr_ref[...], preferred_element_type=F32)
    lh = jnp.dot(x_lo, wr_ref[:, 0:LANES], preferred_element_type=F32)
    logits = hh[:, 0:LANES] + (hh[:, LANES:2 * LANES] + lh) + br_ref[...]
    lane = lax.broadcasted_iota(jnp.int32, logits.shape, 1)
    big = jnp.int32(LANES)
    neg = jnp.float32(-jnp.inf)
    is_g = (lane >= N_EXPERTS) & (lane < N_EXPERTS + N_GROUPS)
    gl = jnp.where(is_g, logits, neg)
    gmax = jnp.max(gl, axis=-1, keepdims=True)
    g_p = 1.0 / jnp.sum(jnp.exp(gl - gmax), axis=-1, keepdims=True)
    g_idx = jnp.min(jnp.where(gl == gmax, lane - N_EXPERTS, big), axis=-1, keepdims=True)
    in_grp = (lane < N_EXPERTS) & ((lane // EXPERTS_PER_GROUP) == g_idx)
    el = jnp.where(in_grp, logits, neg)
    m1 = jnp.max(el, axis=-1, keepdims=True)
    i1 = jnp.min(jnp.where(el == m1, lane, big), axis=-1, keepdims=True)
    el2 = jnp.where(lane == i1, neg, el)
    m2 = jnp.max(el2, axis=-1, keepdims=True)
    i2 = jnp.min(jnp.where(el2 == m2, lane, big), axis=-1, keepdims=True)
    e2 = jnp.exp(m2 - m1)
    w1 = g_p / (1.0 + e2)
    w2 = w1 * e2

    first_lo = i1 < i2
    e_lo = jnp.where(first_lo, i1, i2)
    e_hi = jnp.where(first_lo, i2, i1)
    w_lo = jnp.where(first_lo, w1, w2)
    w_hi = jnp.where(first_lo, w2, w1)
    bucket = g_idx * PAIR_SLOTS + (e_lo % EXPERTS_PER_GROUP) * EXPERTS_PER_GROUP + e_hi % EXPERTS_PER_GROUP

    @pl.when(pl.program_id(0) == 0)
    def _():
        cnt_ref[...] = jnp.zeros(cnt_ref.shape, F32)

    blane = lax.broadcasted_iota(jnp.int32, (tm, N_BUCKETS), 1)
    onehot = blane == bucket
    tri = (lax.broadcasted_iota(jnp.int32, (tm, tm), 0)
           > lax.broadcasted_iota(jnp.int32, (tm, tm), 1))
    earlier = jnp.dot(tri.astype(BF16), onehot.astype(BF16), preferred_element_type=F32)
    base = cnt_ref[0:1, :]
    rank = jnp.sum(jnp.where(onehot, earlier + base, 0.0), axis=-1, keepdims=True)
    cnt_ref[0:1, :] = base + jnp.sum(onehot.astype(F32), axis=0, keepdims=True)
    bits = lambda v: lax.bitcast_convert_type(v, jnp.int32)
    meta_ref[...] = jnp.where(
        lane == META_BUCKET, bucket,
        jnp.where(lane == META_RANK, rank.astype(jnp.int32),
                  jnp.where(lane == META_W_LO, bits(w_lo),
                            jnp.where(lane == META_W_HI, bits(w_hi), 0))))

    x1_ref[...] = x1
    half = x1.shape[1] // 2
    xb = x_hi.astype(F32)
    pk_ref[...] = (bits(xb[:, half:]) & HI16) | lax.shift_right_logical(bits(xb[:, :half]), 16)


def _merge_call(x2d, o, yl, gatt, grec, woa, wol, wout, fg, wr, br, tm):
    T, d = x2d.shape
    assert d == 2 * ROW_PK * LANES
    full = lambda a: pl.BlockSpec(a.shape, lambda i: (0,) * a.ndim)
    tok = lambda w: pl.BlockSpec((tm, w), lambda i: (i, 0))
    return pl.pallas_call(
        _merge_kernel,
        grid=(T // tm,),
        in_specs=[tok(d), tok(d), tok(d), tok(d), tok(d), full(woa), full(wol), full(wout),
                  full(fg), full(wr), full(br)],
        out_specs=[tok(d), tok(d // 2), tok(LANES),
                   pl.BlockSpec((8, N_BUCKETS), lambda i: (0, 0))],
        out_shape=[jax.ShapeDtypeStruct((T, d), F32),
                   jax.ShapeDtypeStruct((T, d // 2), jnp.int32),
                   jax.ShapeDtypeStruct((T, LANES), jnp.int32),
                   jax.ShapeDtypeStruct((8, N_BUCKETS), F32)],
        compiler_params=pltpu.CompilerParams(dimension_semantics=("arbitrary",),
                                             vmem_limit_bytes=VMEM_LIMIT),
        name="merge",
    )(x2d, o, yl, gatt, grec, woa, wol, wout, fg, wr, br)


def _row_copy(src, src_row, dst, dst_row, rows, sem):
    return pltpu.make_async_copy(src.at[pl.ds(src_row * rows, rows)],
                                 dst.at[pl.ds(dst_row * rows, rows)], sem)


def _for_row_groups(n_rows, slot_of, copy_of):
    def start(g, _):
        rows = [g * DMA_GROUP + u for u in range(DMA_GROUP)]
        slots = [slot_of(t) for t in rows]
        for t, s in zip(rows, slots):
            copy_of(t, s).start()
        return 0

    def wait(g, _):
        for u in range(DMA_GROUP):
            copy_of(g * DMA_GROUP + u, 0).wait()
        return 0

    lax.fori_loop(0, n_rows // DMA_GROUP, start, 0)
    lax.fori_loop(0, n_rows // DMA_GROUP, wait, 0)


def _fill_unused_slots(cnt_ref, off_ref, total_ref, zero_ref, xs_ref, sem, R, n_tiles):
    zero_ref[...] = jnp.zeros(zero_ref.shape, zero_ref.dtype)
    bits = R.bit_length() - 1

    def run_copy(pos, size):
        return pltpu.make_async_copy(zero_ref.at[pl.ds(0, size * ROW)],
                                     xs_ref.at[pl.ds(pos * ROW, size * ROW)], sem)

    def bucket_pads(b, op):
        pad = (-cnt_ref[b]) & (R - 1)
        pos = off_ref[b] + cnt_ref[b]
        for k in range(bits):
            has = (pad >> k) & 1

            @pl.when(has == 1)
            def _():
                op(run_copy(pos, 1 << k))

            pos = pos + has * (1 << k)

    def tail_tile(i, op):
        op(run_copy(i * R, R))

    for op in (lambda c: c.start(), lambda c: c.wait()):
        lax.fori_loop(0, N_BUCKETS, lambda b, _: (bucket_pads(b, op), 0)[1], 0)
        lax.fori_loop(total_ref[0], n_tiles, lambda i, _: (tail_tile(i, op), 0)[1], 0)


def _dispatch_kernel(bucket_ref, rank_ref, off_ref, cnt_ref, total_ref, pk_ref, meta_ref, xs_ref,
                     stage_ref, zero_ref, sem, fill_sem, *, tb, R, n_tiles):
    @pl.when(pl.program_id(0) == 0)
    def _():
        stage_ref[...] = jnp.zeros(stage_ref.shape, stage_ref.dtype)
        _fill_unused_slots(cnt_ref, off_ref, total_ref, zero_ref, xs_ref, fill_sem, R, n_tiles)

    for c in range(ROW_PK):
        stage_ref[pl.ds(c, tb, stride=ROW), :] = pk_ref[:, c * LANES:(c + 1) * LANES]
    stage_ref[pl.ds(ROW_PK, tb, stride=ROW), :] = meta_ref[...]
    _for_row_groups(
        tb,
        lambda t: off_ref[bucket_ref[t]] + rank_ref[t],
        lambda t, slot: _row_copy(stage_ref, t, xs_ref, slot, ROW, sem))


def _dispatch_call(bucket, rank, off, cnt, total, pk, meta, n_tiles, R, tb):
    T = bucket.shape[0]
    assert pk.shape[1] == ROW_PK * LANES
    smem_blk = pl.BlockSpec((tb,), lambda i: (i,), memory_space=pltpu.SMEM)
    smem = pl.BlockSpec(memory_space=pltpu.SMEM)
    return pl.pallas_call(
        functools.partial(_dispatch_kernel, tb=tb, R=R, n_tiles=n_tiles),
        grid=(T // tb,),
        in_specs=[smem_blk, smem_blk, smem, smem, smem,
                  pl.BlockSpec((tb, ROW_PK * LANES), lambda i: (i, 0)),
                  pl.BlockSpec((tb, LANES), lambda i: (i, 0))],
        out_specs=pl.BlockSpec(memory_space=pl.ANY),
        out_shape=jax.ShapeDtypeStruct((n_tiles * R * ROW, LANES), jnp.int32),
        scratch_shapes=[pltpu.VMEM((tb * ROW, LANES), jnp.int32),
                        pltpu.VMEM((R * ROW, LANES), jnp.int32),
                        pltpu.SemaphoreType.DMA(()), pltpu.SemaphoreType.DMA(())],
        compiler_params=pltpu.CompilerParams(dimension_semantics=("arbitrary",),
                                             vmem_limit_bytes=VMEM_LIMIT),
        name="dispatch",
    )(bucket, rank, off, cnt, total, pk, meta)


def _expert_kernel(lo_ref, hi_ref, nv_ref, blk_ref, xs_ref, wgl_ref, wul_ref, wdl_ref,
                   wgh_ref, wuh_ref, wdh_ref, ys_ref, *, R):
    nv = nv_ref[pl.program_id(0)]

    @pl.when(nv == 0)
    def _():
        ys_ref[...] = jnp.zeros(ys_ref.shape, F32)

    @pl.when(nv > 0)
    def _():
        f32 = lambda v: lax.bitcast_convert_type(v, F32)
        words = [xs_ref[pl.ds(c, R, stride=ROW), :] for c in range(ROW_PK)]
        xn = jnp.concatenate([f32(w << 16) for w in words] + [f32(w & HI16) for w in words],
                             axis=1).astype(BF16)
        meta = f32(xs_ref[pl.ds(ROW_PK, R, stride=ROW), :])
        lane = lax.broadcasted_iota(jnp.int32, meta.shape, 1)
        w_lo = jnp.sum(jnp.where(lane == META_W_LO, meta, 0.0), axis=-1, keepdims=True)
        w_hi = jnp.sum(jnp.where(lane == META_W_HI, meta, 0.0), axis=-1, keepdims=True)

        def hidden(wg_ref, wu_ref, w):
            gt = jnp.dot(xn, wg_ref[0], preferred_element_type=F32)
            up = jnp.dot(xn, wu_ref[0], preferred_element_type=F32)
            return ((gt * _sigmoid(gt)) * up * w).astype(BF16)

        moe = (jnp.dot(hidden(wgl_ref, wul_ref, w_lo), wdl_ref[0], preferred_element_type=F32)
               + jnp.dot(hidden(wgh_ref, wuh_ref, w_hi), wdh_ref[0], preferred_element_type=F32))
        for c in range(ROW):
            ys_ref[pl.ds(c, R, stride=ROW), :] = moe[:, c * LANES:(c + 1) * LANES]


def _expert_call(tile_lo, tile_hi, tile_nv, tile_blk, xs, wg, wu, wd, R):
    n_tiles = tile_lo.shape[0]
    E, d, de = wg.shape
    assert d == ROW * LANES
    lo_map = lambda i, lo, hi, nv, blk: (lo[i], 0, 0)
    hi_map = lambda i, lo, hi, nv, blk: (hi[i], 0, 0)
    grid_spec = pltpu.PrefetchScalarGridSpec(
        num_scalar_prefetch=4,
        grid=(n_tiles,),
        in_specs=[pl.BlockSpec((R * ROW, LANES), lambda i, lo, hi, nv, blk: (blk[i], 0)),
                  pl.BlockSpec((1, d, de), lo_map), pl.BlockSpec((1, d, de), lo_map),
                  pl.BlockSpec((1, de, d), lo_map),
                  pl.BlockSpec((1, d, de), hi_map), pl.BlockSpec((1, d, de), hi_map),
                  pl.BlockSpec((1, de, d), hi_map)],
        out_specs=pl.BlockSpec((R * ROW, LANES), lambda i, lo, hi, nv, blk: (i, 0)),
    )
    return pl.pallas_call(
        functools.partial(_expert_kernel, R=R),
        grid_spec=grid_spec,
        out_shape=jax.ShapeDtypeStruct((n_tiles * R * ROW, LANES), F32),
        compiler_params=pltpu.CompilerParams(dimension_semantics=("arbitrary",),
                                             vmem_limit_bytes=VMEM_LIMIT),
        name="experts",
    )(tile_lo, tile_hi, tile_nv, tile_blk, xs, wg, wu, wd, wg, wu, wd)


def _return_kernel(bucket_ref, rank_ref, off_ref, ys_ref, x1_ref, fin_ref, y_ref, buf_ref, sem,
                   *, tb):
    _for_row_groups(
        tb,
        lambda t: off_ref[bucket_ref[t]] + rank_ref[t],
        lambda t, slot: _row_copy(ys_ref, slot, buf_ref, t, ROW, sem))
    moe = jnp.concatenate([buf_ref[pl.ds(c, tb, stride=ROW), :] for c in range(ROW)], axis=1)
    y_ref[...] = _rms(x1_ref[...] + moe, fin_ref[...])


def _return_call(bucket, rank, off, ys, x1, fin, tb):
    T, d = x1.shape
    smem_blk = pl.BlockSpec((tb,), lambda i: (i,), memory_space=pltpu.SMEM)
    return pl.pallas_call(
        functools.partial(_return_kernel, tb=tb),
        grid=(T // tb,),
        in_specs=[smem_blk, smem_blk, pl.BlockSpec(memory_space=pltpu.SMEM),
                  pl.BlockSpec(memory_space=pl.ANY),
                  pl.BlockSpec((tb, d), lambda i: (i, 0)),
                  pl.BlockSpec(fin.shape, lambda i: (0, 0))],
        out_specs=pl.BlockSpec((tb, d), lambda i: (i, 0)),
        out_shape=jax.ShapeDtypeStruct((T, d), F32),
        scratch_shapes=[pltpu.VMEM((tb * ROW, LANES), F32), pltpu.SemaphoreType.DMA(())],
        compiler_params=pltpu.CompilerParams(dimension_semantics=("arbitrary",),
                                             vmem_limit_bytes=VMEM_LIMIT),
        name="moe_return",
    )(bucket, rank, off, ys, x1, fin)


def _routing_tables(counts, T, R):
    n_tiles = T // R + N_GROUPS * (EXPERTS_PER_GROUP * (EXPERTS_PER_GROUP - 1) // 2)
    cnt = counts.astype(jnp.int32)
    tiles_b = (cnt + R - 1) // R
    tile_end = jnp.cumsum(tiles_b)
    tile_start = tile_end - tiles_b
    off = tile_start * R
    total = tile_end[-1]
    i = jnp.arange(n_tiles, dtype=jnp.int32)
    blk = jnp.minimum(i, total - 1)
    b = jnp.searchsorted(tile_end, blk, side="right").astype(jnp.int32)
    nv = jnp.where(i < total, jnp.clip(cnt[b] - (blk - tile_start[b]) * R, 0, R), 0)
    grp = (b // PAIR_SLOTS) * EXPERTS_PER_GROUP
    lo = grp + (b % PAIR_SLOTS) // EXPERTS_PER_GROUP
    hi = grp + b % EXPERTS_PER_GROUP
    return cnt, off, total.reshape(1), lo, hi, nv.astype(jnp.int32), blk, n_tiles


def _rope_tables(S):
    inv = ROPE_THETA ** (-jnp.arange(0, QK_ROPE, 2, dtype=F32) / QK_ROPE)
    ang = jnp.arange(S, dtype=F32)[:, None] * inv[None, :]
    cos, sin = jnp.cos(ang), jnp.sin(ang)
    half = QK_ROPE // 2
    z = lambda n: jnp.zeros((S, n), F32)
    cos_t = jnp.concatenate([cos, cos, z(LANES - QK_ROPE)], axis=1)
    nsin_lo = jnp.concatenate([-sin, z(LANES - half)], axis=1)
    sin_hi = jnp.concatenate([z(half), sin, z(LANES - QK_ROPE)], axis=1)
    return cos_t, nsin_lo, sin_hi


def _prep_params(p):
    d = p["w_in"].shape[0]
    row = lambda a: a.reshape(1, -1).astype(F32)
    w_in = p["w_in"]
    c_q, c_kv, c_r = Q_LORA, Q_LORA + KV_LORA, Q_LORA + KV_LORA + QK_ROPE
    win = jnp.concatenate(
        [w_in[:, :c_r], jnp.zeros((d, LANES - QK_ROPE), F32), w_in[:, c_r:]], axis=1).astype(BF16)
    wqb = p["w_q_b"].reshape(Q_LORA, N_HEADS, QK_NOPE + QK_ROPE)
    wqb = jnp.pad(wqb, ((0, 0), (0, 0), (0, HEAD_PAD - QK_NOPE - QK_ROPE)))
    wqb = wqb.reshape(Q_LORA, N_HEADS * HEAD_PAD).astype(BF16)
    wkvb = p["w_kv_b"].reshape(KV_LORA, N_HEADS, QK_NOPE + V_HEAD)
    wkvb = jnp.concatenate([wkvb[:, :, :QK_NOPE].reshape(KV_LORA, -1),
                            wkvb[:, :, QK_NOPE:].reshape(KV_LORA, -1)], axis=1).astype(BF16)
    wlru = jnp.concatenate([p["lru_a_w"], p["lru_x_w"]], axis=-1).astype(BF16)
    wr = jnp.concatenate([p["router_expert_w"], p["router_group_w"],
                          jnp.zeros((d, LANES - N_EXPERTS - N_GROUPS), F32)], axis=1)
    wr_hi = wr.astype(BF16)
    wr = jnp.concatenate([wr_hi, (wr - wr_hi.astype(F32)).astype(BF16)], axis=1)
    br =jnp.concatenate([p["router_expert_b"], p["router_group_b"],
                          jnp.zeros((LANES - N_EXPERTS - N_GROUPS,), F32)]).reshape(1, LANES)
    return dict(
        mixg=row(p["mix_norm"]), win=win, qg=row(p["q_a_norm"]), wqb=wqb, kvg=row(p["kv_a_norm"]),
        wkvb=wkvb, gateb=p["gate_b"].astype(F32), cw=p["conv_w"].astype(F32), cb=row(p["conv_b"]),
        wlru=wlru, ab=p["lru_a_b"].astype(F32), xb=p["lru_x_b"].astype(F32),
        lam=p["lru_lambda"].astype(F32), woa=p["w_o_attn"].astype(BF16),
        wol=p["w_o_lru"].astype(BF16), wout=p["w_out"].astype(BF16), fg=row(p["ffn_norm"]),
        wr=wr, br=br, wg=p["w_gate_e"].astype(BF16), wu=p["w_up_e"].astype(BF16),
        wd=p["w_down_e"].astype(BF16), fin=row(p["final_norm"]))


def _trunk(x, w):
    B, S, d = x.shape
    T = B * S
    x2d = x.reshape(T, d)
    tm = min(512, S)
    tabs = _rope_tables(S)
    q, k, v, xc, gelu_g, gatt, grec = _proj_call(
        x2d, B, S, tabs, w["mixg"], w["win"], w["qg"], w["wqb"], w["kvg"], w["wkvb"], w["gateb"],
        w["cw"], w["cb"], tm)
    o = _attn_call(q, k, v, tq=min(1024, S), tk=min(1024, S), unroll=4)

    L = min(256, S)
    xc3 = xc.reshape(B, S, d)
    gelu3 = gelu_g.reshape(B, S, d)
    hb = _lru_call(xc3, w["wlru"][1], w["ab"][1:2], w["xb"][1:2], w["lam"][1:2], None, None, L,
                   reverse=True)
    yl = _lru_call(xc3, w["wlru"][0], w["ab"][0:1], w["xb"][0:1], w["lam"][0:1], hb, gelu3, L,
                   reverse=False)

    x1, pk, meta, cnt = _merge_call(x2d, o.reshape(T, d), yl.reshape(T, d), gatt, grec, w["woa"],
                                    w["wol"], w["wout"], w["fg"], w["wr"], w["br"], tm=min(512, S))
    R = 256
    cnt, off, total, lo, hi, nv, blk, n_tiles = _routing_tables(cnt[0], T, R)
    bucket, rank = meta[:, META_BUCKET], meta[:, META_RANK]
    xs = _dispatch_call(bucket, rank, off, cnt, total, pk, meta, n_tiles, R, tb=min(1024, T))
    ys = _expert_call(lo, hi, nv, blk, xs, w["wg"], w["wu"], w["wd"], R)
    y = _return_call(bucket, rank, off, ys, x1, w["fin"], tb=min(1024, T))
    return y.reshape(B, S, d)


def kernel(x_prompt, x_sample, mix_norm, w_in, q_a_norm, w_q_b, kv_a_norm, w_kv_b, w_o_attn, conv_w, conv_b, lru_a_w, lru_a_b, lru_x_w, lru_x_b, lru_lambda, w_o_lru, gate_b, w_out, ffn_norm, router_group_w, router_group_b, router_expert_w, router_expert_b, w_gate_e, w_up_e, w_down_e, final_norm):
    depth = mix_norm.shape[0]
    layers = []
    for l in range(depth):
        layers.append(_prep_params(dict(
            mix_norm=mix_norm[l], w_in=w_in[l], q_a_norm=q_a_norm[l], w_q_b=w_q_b[l],
            kv_a_norm=kv_a_norm[l], w_kv_b=w_kv_b[l], w_o_attn=w_o_attn[l], conv_w=conv_w[l],
            conv_b=conv_b[l], lru_a_w=lru_a_w[l], lru_a_b=lru_a_b[l], lru_x_w=lru_x_w[l],
            lru_x_b=lru_x_b[l], lru_lambda=lru_lambda[l], w_o_lru=w_o_lru[l], gate_b=gate_b[l],
            w_out=w_out[l], ffn_norm=ffn_norm[l], router_group_w=router_group_w[l],
            router_group_b=router_group_b[l], router_expert_w=router_expert_w[l],
            router_expert_b=router_expert_b[l], w_gate_e=w_gate_e[l], w_up_e=w_up_e[l],
            w_down_e=w_down_e[l], final_norm=final_norm)))
    assert depth == 1

    def trunk(x):
        for w in layers:
            x = _trunk(x, w)
        return x

    return (trunk(x_prompt), trunk(x_sample))
```

```python
import functools
import math

import jax
import jax.numpy as jnp
from jax import lax
from jax.experimental import pallas as pl
from jax.experimental.pallas import tpu as pltpu

N_HEADS = 8
QK_NOPE = 128
QK_ROPE = 64
V_HEAD = 128
Q_LORA = 384
KV_LORA = 256
ROPE_THETA = 10000.0
LRU_BLOCKS = 8
LRU_C = 8.0
N_GROUPS = 4
EXPERTS_PER_GROUP = 8
N_EXPERTS = N_GROUPS * EXPERTS_PER_GROUP
EPS = 1e-6

LANES = 128
HEAD_PAD = 256
LAT_COLS = Q_LORA + KV_LORA + LANES
PAIR_SLOTS = EXPERTS_PER_GROUP * EXPERTS_PER_GROUP
N_BUCKETS = N_GROUPS * PAIR_SLOTS
ROW = 8
ROW_PK = 4
META_BUCKET, META_RANK, META_W_LO, META_W_HI = 0, 1, 2, 3
HI16 = -65536
DMA_GROUP = 16
SQRT_FLOOR = 1e-36
VMEM_LIMIT = 56 * 1024 * 1024

F32 = jnp.float32
BF16 = jnp.bfloat16


def _sigmoid(z):
    return 1.0 / (1.0 + jnp.exp2(z * (-math.log2(math.e))))


def _rms(x, g):
    var = jnp.mean(x * x, axis=-1, keepdims=True)
    return x * lax.rsqrt(var + EPS) * g


def _rope(pe, cos_t, nsin_lo, sin_hi):
    x2_to_lo = pltpu.roll(pe, 96, axis=1)
    x1_to_hi = pltpu.roll(pe, 32, axis=1)
    return pe * cos_t + x2_to_lo * nsin_lo + x1_to_hi * sin_hi


def _conv_time(x, prev, nxt, cw_ref, cb_ref):
    L, W = x.shape
    row8 = lax.broadcasted_iota(jnp.int32, (8, W), 0)

    def shifted(k, head, tail):
        r = pltpu.roll(x, k % L, axis=0)
        return jnp.concatenate([head(r[0:8]), r[8:L - 8], tail(r[L - 8:L])], axis=0)

    keep = lambda s: s
    xm1 = shifted(1, lambda s: jnp.where(row8 == 0, prev[7:8, :], s), keep)
    xm2 = shifted(2, lambda s: jnp.where(row8 == 0, prev[6:7, :],
                                         jnp.where(row8 == 1, prev[7:8, :], s)), keep)
    xp1 = shifted(-1, keep, lambda s: jnp.where(row8 == 7, nxt[0:1, :], s))
    return cb_ref[...] + (xm2 * cw_ref[0:1, :] + xm1 * cw_ref[1:2, :] + x * cw_ref[2:3, :]
                          + xp1 * cw_ref[3:4, :])


def _proj_kernel(x_ref, xp_ref, xn_ref, cos_ref, nsl_ref, sh_ref, mixg_ref, win_ref, qg_ref,
                 wqb_ref, kvg_ref, wkvb_ref, gateb_ref, cw_ref, cb_ref, q_ref, k_ref, v_ref,
                 xc_ref, gelu_ref, gatt_ref, grec_ref, *, scale, nt_seq):
    d = x_ref.shape[-1]
    xn = _rms(x_ref[...], mixg_ref[...]).astype(BF16)
    lat = jnp.dot(xn, win_ref[:, 0:LAT_COLS], preferred_element_type=F32)
    cos_t, nsl, sh = cos_ref[...], nsl_ref[...], sh_ref[...]

    qn = _rms(lat[:, 0:Q_LORA], qg_ref[...]).astype(BF16)
    qf = jnp.dot(qn, wqb_ref[...], preferred_element_type=F32)
    for h in range(N_HEADS):
        base = h * HEAD_PAD
        q_ref[0, h, :, 0:LANES] = (qf[:, base:base + LANES] * scale).astype(BF16)
        pe = _rope(qf[:, base + LANES:base + HEAD_PAD], cos_t, nsl, sh)
        q_ref[0, h, :, LANES:HEAD_PAD] = (pe * scale).astype(BF16)

    kvn = _rms(lat[:, Q_LORA:Q_LORA + KV_LORA], kvg_ref[...]).astype(BF16)
    kvf = jnp.dot(kvn, wkvb_ref[...], preferred_element_type=F32)
    kpe = _rope(lat[:, Q_LORA + KV_LORA:LAT_COLS], cos_t, nsl, sh).astype(BF16)
    for h in range(N_HEADS):
        k_ref[0, h, :, 0:LANES] = kvf[:, h * LANES:(h + 1) * LANES].astype(BF16)
        k_ref[0, h, :, LANES:HEAD_PAD] = kpe
        v_ref[0, h, :, :] = kvf[:, (N_HEADS + h) * LANES:(N_HEADS + h + 1) * LANES].astype(BF16)

    c0 = LAT_COLS
    ti = pl.program_id(0) % nt_seq
    halo = jnp.concatenate([xp_ref[...], xn_ref[...]], axis=0)
    halo = jnp.dot(_rms(halo, mixg_ref[...]).astype(BF16), win_ref[:, c0:c0 + d],
                   preferred_element_type=F32)
    prev = jnp.where(ti > 0, halo[0:8], 0.0)
    nxt = jnp.where(ti < nt_seq - 1, halo[8:16], 0.0)
    xl = jnp.dot(xn, win_ref[:, c0:c0 + d], preferred_element_type=F32)
    xc_ref[...] = _conv_time(xl, prev, nxt, cw_ref, cb_ref)
    g = jnp.dot(xn, win_ref[:, c0 + d:c0 + 2 * d], preferred_element_type=F32)
    gelu_ref[...] = jax.nn.gelu(g).astype(BF16)
    ga = jnp.dot(xn, win_ref[:, c0 + 2 * d:c0 + 3 * d], preferred_element_type=F32)
    gatt_ref[...] = _sigmoid(ga + gateb_ref[0:1, :]).astype(BF16)
    gb = jnp.dot(xn, win_ref[:, c0 + 3 * d:c0 + 4 * d], preferred_element_type=F32)
    grec_ref[...] = _sigmoid(gb + gateb_ref[1:2, :]).astype(BF16)


def _proj_call(x2d, B, S, tabs, mixg, win, qg, wqb, kvg, wkvb, gateb, cw, cb, tm):
    T, d = x2d.shape
    nt_seq = S // tm
    r8 = tm // 8
    halo_prev = pl.BlockSpec((8, d), lambda i: (jnp.maximum(i * r8 - 1, 0), 0))
    halo_next = pl.BlockSpec((8, d), lambda i: (jnp.minimum((i + 1) * r8, T // 8 - 1), 0))
    full = lambda a: pl.BlockSpec(a.shape, lambda i: (0,) * a.ndim, pipeline_mode=pl.Buffered(1))
    tab_spec = pl.BlockSpec((tm, LANES), lambda i: (i % nt_seq, 0))
    tok = lambda w: pl.BlockSpec((tm, w), lambda i: (i, 0))
    head_spec = lambda w: pl.BlockSpec((1, N_HEADS, tm, w), lambda i: (i // nt_seq, 0, i % nt_seq, 0))
    scale = math.log2(math.e) / math.sqrt(QK_NOPE + QK_ROPE)
    return pl.pallas_call(
        functools.partial(_proj_kernel, scale=scale, nt_seq=nt_seq),
        grid=(T // tm,),
        in_specs=[tok(d), halo_prev, halo_next, tab_spec, tab_spec, tab_spec, full(mixg), full(win),
                  full(qg), full(wqb), full(kvg), full(wkvb), full(gateb), full(cw), full(cb)],
        out_specs=[head_spec(HEAD_PAD), head_spec(HEAD_PAD), head_spec(V_HEAD),
                   tok(d), tok(d), tok(d), tok(d)],
        out_shape=[
            jax.ShapeDtypeStruct((B, N_HEADS, S, HEAD_PAD), BF16),
            jax.ShapeDtypeStruct((B, N_HEADS, S, HEAD_PAD), BF16),
            jax.ShapeDtypeStruct((B, N_HEADS, S, V_HEAD), BF16),
            jax.ShapeDtypeStruct((T, d), F32),
            jax.ShapeDtypeStruct((T, d), BF16),
            jax.ShapeDtypeStruct((T, d), BF16),
            jax.ShapeDtypeStruct((T, d), BF16),
        ],
        compiler_params=pltpu.CompilerParams(dimension_semantics=("arbitrary",),
                                             vmem_limit_bytes=VMEM_LIMIT),
        name="proj",
    )(x2d, x2d, x2d, *tabs, mixg, win, qg, wqb, kvg, wkvb, gateb, cw, cb)


def _attn_kernel(q_ref, k_ref, v_ref, o_ref, s_ref, m_ref, acc_ref, *, tk, unroll):
    q = q_ref[0, 0]
    nk = k_ref.shape[2] // tk
    n_rep = tk // LANES

    def scores(j):
        off = j * tk if isinstance(j, int) else pl.multiple_of(j * tk, tk)
        kc = k_ref[0, 0, pl.ds(off, tk), :]
        return lax.dot_general(q, kc, (((1,), (1,)), ((), ())), preferred_element_type=F32)

    ones_col = (lax.broadcasted_iota(jnp.int32, (tk, LANES), 1) == 0).astype(BF16)

    def consume(j, s):
        off = j * tk if isinstance(j, int) else pl.multiple_of(j * tk, tk)
        m_old = m_ref[...]
        m_new = jnp.maximum(m_old, jnp.max(s, axis=-1, keepdims=True))
        alpha = jnp.exp2(m_old - m_new)
        p = jnp.exp2(s - jnp.tile(m_new, (1, n_rep)))
        v1 = jnp.concatenate([v_ref[0, 0, pl.ds(off, tk), :], ones_col], axis=1)
        pv = jnp.dot(p.astype(BF16), v1, preferred_element_type=F32)
        acc_ref[...] = jnp.tile(alpha, (1, 2)) * acc_ref[...] + pv
        m_ref[...] = m_new

    m_ref[...] = jnp.full(m_ref.shape, -jnp.inf, F32)
    acc_ref[...] = jnp.zeros(acc_ref.shape, F32)
    s_ref[0] = scores(0)

    def step(j, cur):
        s = s_ref[cur]
        s_ref[1 - cur] = scores(j + 1)
        consume(j, s)

    if nk <= unroll:
        for j in range(nk - 1):
            step(j, j % 2)
    else:
        assert unroll % 2 == 0 and nk % unroll == 0
        def group(g, _):
            for u in range(unroll):
                step(unroll * g + u, u % 2)
            return 0
        lax.fori_loop(0, nk // unroll - 1, group, 0)
        for j in range(nk - unroll, nk - 1):
            step(j, j % 2)
    consume(nk - 1, s_ref[(nk - 1) % 2])
    l = jnp.sum(acc_ref[:, V_HEAD:2 * V_HEAD], axis=-1, keepdims=True)
    o_ref[0] = (acc_ref[:, 0:V_HEAD] / l).astype(BF16)


def _attn_call(q, k, v, tq, tk, unroll=8):
    B, H, S, _ = q.shape
    return pl.pallas_call(
        functools.partial(_attn_kernel, tk=tk, unroll=unroll),
        grid=(B, H, S // tq),
        in_specs=[
            pl.BlockSpec((1, 1, tq, HEAD_PAD), lambda b, h, i: (b, h, i, 0)),
            pl.BlockSpec((1, 1, S, HEAD_PAD), lambda b, h, i: (b, h, 0, 0)),
            pl.BlockSpec((1, 1, S, V_HEAD), lambda b, h, i: (b, h, 0, 0)),
        ],
        out_specs=pl.BlockSpec((1, tq, V_HEAD), lambda b, h, i: (b, i, h)),
        out_shape=jax.ShapeDtypeStruct((B, S, H * V_HEAD), BF16),
        scratch_shapes=[pltpu.VMEM((2, tq, tk), F32), pltpu.VMEM((tq, LANES), F32),
                        pltpu.VMEM((tq, 2 * V_HEAD), F32)],
        compiler_params=pltpu.CompilerParams(
            dimension_semantics=("arbitrary", "arbitrary", "arbitrary"),
            vmem_limit_bytes=VMEM_LIMIT),
        name="attn",
    )(q, k, v)


def _lru_kernel(*refs, reverse):
    if reverse:
        (x_ref, w_ref, ab_ref, xb_ref, lam_ref, out_ref, h_ref, a_s, b_s, h_s) = refs
    else:
        (x_ref, w_ref, ab_ref, xb_ref, lam_ref, hb_ref, gelu_ref,
         out_ref, h_ref, a_s, b_s, h_s) = refs
    L, W = x_ref.shape[1], x_ref.shape[2]
    blk = W // LRU_BLOCKS

    @pl.when(pl.program_id(1) == 0)
    def _():
        h_ref[...] = jnp.zeros(h_ref.shape, F32)

    xc = x_ref[0]
    xcb = xc.astype(BF16)
    sp = jnp.log1p(jnp.exp(-jnp.abs(lam_ref[...]))) + jnp.maximum(-lam_ref[...], 0.0)
    sp2 = sp * (-LRU_C * math.log2(math.e))
    for n in range(LRU_BLOCKS):
        sl = slice(n * blk, (n + 1) * blk)
        z = jnp.dot(xcb[:, sl], w_ref[n], preferred_element_type=F32)
        r = _sigmoid(z[:, 0:blk] + ab_ref[:, sl])
        i = _sigmoid(z[:, blk:2 * blk] + xb_ref[:, sl])
        a = jnp.exp2(r * sp2[:, sl])
        a_s[:, sl] = a
        om = 1.0 - a * a
        root = om * lax.rsqrt(jnp.maximum(om, SQRT_FLOOR))
        b_s[:, sl] = root * (i * xc[:, sl])

    def step(t, h):
        tt = (L - 1 - t) if reverse else t
        h = a_s[pl.ds(tt, 1), :] * h + b_s[pl.ds(tt, 1), :]
        h_s[pl.ds(tt, 1), :] = h
        return h

    h_ref[...] = lax.fori_loop(0, L, step, h_ref[...], unroll=8)
    if reverse:
        out_ref[0] = h_s[...].astype(BF16)
    else:
        hsum = h_s[...] + hb_ref[0].astype(F32)
        out_ref[0] = (hsum * gelu_ref[0].astype(F32)).astype(BF16)


def _lru_call(xc, w, ab, xb, lam, hb, gelu, L, reverse):
    B, S, W = xc.shape
    nC = S // L
    pos = (lambda j: nC - 1 - j) if reverse else (lambda j: j)
    full = lambda a: pl.BlockSpec(a.shape, lambda b, j: (0,) * a.ndim)
    chunk = pl.BlockSpec((1, L, W), lambda b, j: (b, pos(j), 0))
    in_specs = [chunk, full(w), full(ab), full(xb), full(lam)]
    args = [xc, w, ab, xb, lam]
    if not reverse:
        in_specs += [chunk, chunk]
        args += [hb, gelu]
    return pl.pallas_call(
        functools.partial(_lru_kernel, reverse=reverse),
        grid=(B, nC),
        in_specs=in_specs,
        out_specs=chunk,
        out_shape=jax.ShapeDtypeStruct((B, S, W), BF16),
        scratch_shapes=[pltpu.VMEM((1, W), F32), pltpu.VMEM((L, W), F32), pltpu.VMEM((L, W), F32),
                        pltpu.VMEM((L, W), F32)],
        compiler_params=pltpu.CompilerParams(dimension_semantics=("arbitrary", "arbitrary"),
                                             vmem_limit_bytes=VMEM_LIMIT),
        name="lru_bwd" if reverse else "lru_fwd",
    )(*args)


def _merge_kernel(x_ref, o_ref, yl_ref, gatt_ref, grec_ref, woa_ref, wol_ref, wout_ref, fg_ref,
                  wr_ref, br_ref, x1_ref, pk_ref, meta_ref, cnt_ref):
    tm = x_ref.shape[0]
    y_attn = jnp.dot(o_ref[...], woa_ref[...], preferred_element_type=F32)
    y_lru = jnp.dot(yl_ref[...], wol_ref[...], preferred_element_type=F32)
    mix = gatt_ref[...].astype(F32) * y_attn + grec_ref[...].astype(F32) * y_lru
    x1 = x_ref[...] + jnp.dot(mix.astype(BF16), wout_ref[...], preferred_element_type=F32)
    xn2 = _rms(x1, fg_ref[...])

    x_hi = xn2.astype(BF16)
    x_lo = (xn2 - x_hi.astype(F32)).astype(BF16)
    hh = jnp.dot(x_hi, wr_ref[...], preferred_element_type=F32)
    lh = jnp.dot(x_lo, wr_ref[:, 0:LANES], preferred_element_type=F32)
    logits = hh[:, 0:LANES] + (hh[:, LANES:2 * LANES] + lh) + br_ref[...]
    lane = lax.broadcasted_iota(jnp.int32, logits.shape, 1)
    big = jnp.int32(LANES)
    neg = jnp.float32(-jnp.inf)
    is_g = (lane >= N_EXPERTS) & (lane < N_EXPERTS + N_GROUPS)
    gl = jnp.where(is_g, logits, neg)
    gmax = jnp.max(gl, axis=-1, keepdims=True)
    g_p = 1.0 / jnp.sum(jnp.exp(gl - gmax), axis=-1, keepdims=True)
    g_idx = jnp.min(jnp.where(gl == gmax, lane - N_EXPERTS, big), axis=-1, keepdims=True)
    in_grp = (lane < N_EXPERTS) & ((lane // EXPERTS_PER_GROUP) == g_idx)
    el = jnp.where(in_grp, logits, neg)
    m1 = jnp.max(el, axis=-1, keepdims=True)
    i1 = jnp.min(jnp.where(el == m1, lane, big), axis=-1, keepdims=True)
    el2 = jnp.where(lane == i1, neg, el)
    m2 = jnp.max(el2, axis=-1, keepdims=True)
    i2 = jnp.min(jnp.where(el2 == m2, lane, big), axis=-1, keepdims=True)
    e2 = jnp.exp(m2 - m1)
    w1 = g_p / (1.0 + e2)
    w2 = w1 * e2

    first_lo = i1 < i2
    e_lo = jnp.where(first_lo, i1, i2)
    e_hi = jnp.where(first_lo, i2, i1)
    w_lo = jnp.where(first_lo, w1, w2)
    w_hi = jnp.where(first_lo, w2, w1)
    bucket = g_idx * PAIR_SLOTS + (e_lo % EXPERTS_PER_GROUP) * EXPERTS_PER_GROUP + e_hi % EXPERTS_PER_GROUP

    @pl.when(pl.program_id(0) == 0)
    def _():
        cnt_ref[...] = jnp.zeros(cnt_ref.shape, F32)

    blane = lax.broadcasted_iota(jnp.int32, (tm, N_BUCKETS), 1)
    onehot = blane == bucket
    tri = (lax.broadcasted_iota(jnp.int32, (tm, tm), 0)
           > lax.broadcasted_iota(jnp.int32, (tm, tm), 1))
    earlier = jnp.dot(tri.astype(BF16), onehot.astype(BF16), preferred_element_type=F32)
    base = cnt_ref[0:1, :]
    rank = jnp.sum(jnp.where(onehot, earlier + base, 0.0), axis=-1, keepdims=True)
    cnt_ref[0:1, :] = base + jnp.sum(onehot.astype(F32), axis=0, keepdims=True)
    bits = lambda v: lax.bitcast_convert_type(v, jnp.int32)
    meta_ref[...] = jnp.where(
        lane == META_BUCKET, bucket,
        jnp.where(lane == META_RANK, rank.astype(jnp.int32),
                  jnp.where(lane == META_W_LO, bits(w_lo),
                            jnp.where(lane == META_W_HI, bits(w_hi), 0))))

    x1_ref[...] = x1
    half = x1.shape[1] // 2
    xb = x_hi.astype(F32)
    pk_ref[...] = (bits(xb[:, half:]) & HI16) | lax.shift_right_logical(bits(xb[:, :half]), 16)


def _merge_call(x2d, o, yl, gatt, grec, woa, wol, wout, fg, wr, br, tm):
    T, d = x2d.shape
    assert d == 2 * ROW_PK * LANES
    full = lambda a: pl.BlockSpec(a.shape, lambda i: (0,) * a.ndim)
    tok = lambda w: pl.BlockSpec((tm, w), lambda i: (i, 0))
    return pl.pallas_call(
        _merge_kernel,
        grid=(T // tm,),
        in_specs=[tok(d), tok(d), tok(d), tok(d), tok(d), full(woa), full(wol), full(wout),
                  full(fg), full(wr), full(br)],
        out_specs=[tok(d), tok(d // 2), tok(LANES),
                   pl.BlockSpec((8, N_BUCKETS), lambda i: (0, 0))],
        out_shape=[jax.ShapeDtypeStruct((T, d), F32),
                   jax.ShapeDtypeStruct((T, d // 2), jnp.int32),
                   jax.ShapeDtypeStruct((T, LANES), jnp.int32),
                   jax.ShapeDtypeStruct((8, N_BUCKETS), F32)],
        compiler_params=pltpu.CompilerParams(dimension_semantics=("arbitrary",),
                                             vmem_limit_bytes=VMEM_LIMIT),
        name="merge",
    )(x2d, o, yl, gatt, grec, woa, wol, wout, fg, wr, br)


def _row_copy(src, src_row, dst, dst_row, rows, sem):
    return pltpu.make_async_copy(src.at[pl.ds(src_row * rows, rows)],
                                 dst.at[pl.ds(dst_row * rows, rows)], sem)


def _start_rows(n_rows, slot_ref, copy_of):
    def start(g, _):
        rows = [g * DMA_GROUP + u for u in range(DMA_GROUP)]
        slots = [slot_ref[t] for t in rows]
        for t, s in zip(rows, slots):
            copy_of(t, s).start()
        return 0

    lax.fori_loop(0, n_rows // DMA_GROUP, start, 0)


def _wait_rows(n_rows, copy_of):
    def wait(g, _):
        for u in range(DMA_GROUP):
            copy_of(g * DMA_GROUP + u, 0).wait()
        return 0

    lax.fori_loop(0, n_rows // DMA_GROUP, wait, 0)


def _fill_unused_slots(cnt_ref, off_ref, total_ref, zero_ref, xs_ref, sem, R, n_tiles):
    zero_ref[...] = jnp.zeros(zero_ref.shape, zero_ref.dtype)
    bits = R.bit_length() - 1

    def run_copy(pos, size):
        return pltpu.make_async_copy(zero_ref.at[pl.ds(0, size * ROW)],
                                     xs_ref.at[pl.ds(pos * ROW, size * ROW)], sem)

    def bucket_pads(b, op):
        pad = (-cnt_ref[b]) & (R - 1)
        pos = off_ref[b] + cnt_ref[b]
        for k in range(bits):
            has = (pad >> k) & 1

            @pl.when(has == 1)
            def _():
                op(run_copy(pos, 1 << k))

            pos = pos + has * (1 << k)

    def tail_tile(i, op):
        op(run_copy(i * R, R))

    for op in (lambda c: c.start(), lambda c: c.wait()):
        lax.fori_loop(0, N_BUCKETS, lambda b, _: (bucket_pads(b, op), 0)[1], 0)
        lax.fori_loop(total_ref[0], n_tiles, lambda i, _: (tail_tile(i, op), 0)[1], 0)


def _dispatch_kernel(slot_ref, off_ref, cnt_ref, total_ref, pk_ref, meta_ref, xs_ref,
                     stage_ref, zero_ref, sem, fill_sem, *, tb, R, n_tiles):
    @pl.when(pl.program_id(0) == 0)
    def _():
        stage_ref[...] = jnp.zeros(stage_ref.shape, stage_ref.dtype)
        _fill_unused_slots(cnt_ref, off_ref, total_ref, zero_ref, xs_ref, fill_sem, R, n_tiles)

    for c in range(ROW_PK):
        stage_ref[pl.ds(c, tb, stride=ROW), :] = pk_ref[:, c * LANES:(c + 1) * LANES]
    stage_ref[pl.ds(ROW_PK, tb, stride=ROW), :] = meta_ref[...]
    copy = lambda t, slot: _row_copy(stage_ref, t, xs_ref, slot, ROW, sem)
    _start_rows(tb, slot_ref, copy)
    _wait_rows(tb, copy)


def _dispatch_call(slot, off, cnt, total, pk, meta, n_tiles, R, tb):
    T = slot.shape[0]
    assert pk.shape[1] == ROW_PK * LANES
    smem_blk = pl.BlockSpec((tb,), lambda i: (i,), memory_space=pltpu.SMEM)
    smem = pl.BlockSpec(memory_space=pltpu.SMEM)
    return pl.pallas_call(
        functools.partial(_dispatch_kernel, tb=tb, R=R, n_tiles=n_tiles),
        grid=(T // tb,),
        in_specs=[smem_blk, smem, smem, smem,
                  pl.BlockSpec((tb, ROW_PK * LANES), lambda i: (i, 0)),
                  pl.BlockSpec((tb, LANES), lambda i: (i, 0))],
        out_specs=pl.BlockSpec(memory_space=pl.ANY),
        out_shape=jax.ShapeDtypeStruct((n_tiles * R * ROW, LANES), jnp.int32),
        scratch_shapes=[pltpu.VMEM((tb * ROW, LANES), jnp.int32),
                        pltpu.VMEM((R * ROW, LANES), jnp.int32),
                        pltpu.SemaphoreType.DMA(()), pltpu.SemaphoreType.DMA(())],
        compiler_params=pltpu.CompilerParams(dimension_semantics=("arbitrary",),
                                             vmem_limit_bytes=VMEM_LIMIT),
        name="dispatch",
    )(slot, off, cnt, total, pk, meta)


def _expert_kernel(lo_ref, hi_ref, nv_ref, blk_ref, xs_ref, wgl_ref, wul_ref, wdl_ref,
                   wgh_ref, wuh_ref, wdh_ref, ys_ref, *, R):
    nv = nv_ref[pl.program_id(0)]

    @pl.when(nv == 0)
    def _():
        ys_ref[...] = jnp.zeros(ys_ref.shape, F32)

    @pl.when(nv > 0)
    def _():
        f32 = lambda v: lax.bitcast_convert_type(v, F32)
        words = [xs_ref[pl.ds(c, R, stride=ROW), :] for c in range(ROW_PK)]
        xn = jnp.concatenate([f32(w << 16) for w in words] + [f32(w & HI16) for w in words],
                             axis=1).astype(BF16)
        meta = f32(xs_ref[pl.ds(ROW_PK, R, stride=ROW), :])
        lane = lax.broadcasted_iota(jnp.int32, meta.shape, 1)
        w_lo = jnp.sum(jnp.where(lane == META_W_LO, meta, 0.0), axis=-1, keepdims=True)
        w_hi = jnp.sum(jnp.where(lane == META_W_HI, meta, 0.0), axis=-1, keepdims=True)

        def hidden(wg_ref, wu_ref, w):
            gt = jnp.dot(xn, wg_ref[0], preferred_element_type=F32)
            up = jnp.dot(xn, wu_ref[0], preferred_element_type=F32)
            return ((gt * _sigmoid(gt)) * up * w).astype(BF16)

        moe = (jnp.dot(hidden(wgl_ref, wul_ref, w_lo), wdl_ref[0], preferred_element_type=F32)
               + jnp.dot(hidden(wgh_ref, wuh_ref, w_hi), wdh_ref[0], preferred_element_type=F32))
        for c in range(ROW):
            ys_ref[pl.ds(c, R, stride=ROW), :] = moe[:, c * LANES:(c + 1) * LANES]


def _expert_call(tile_lo, tile_hi, tile_nv, tile_blk, xs, wg, wu, wd, R):
    n_tiles = tile_lo.shape[0]
    E, d, de = wg.shape
    assert d == ROW * LANES
    lo_map = lambda i, lo, hi, nv, blk: (lo[i], 0, 0)
    hi_map = lambda i, lo, hi, nv, blk: (hi[i], 0, 0)
    grid_spec = pltpu.PrefetchScalarGridSpec(
        num_scalar_prefetch=4,
        grid=(n_tiles,),
        in_specs=[pl.BlockSpec((R * ROW, LANES), lambda i, lo, hi, nv, blk: (blk[i], 0)),
                  pl.BlockSpec((1, d, de), lo_map), pl.BlockSpec((1, d, de), lo_map),
                  pl.BlockSpec((1, de, d), lo_map),
                  pl.BlockSpec((1, d, de), hi_map), pl.BlockSpec((1, d, de), hi_map),
                  pl.BlockSpec((1, de, d), hi_map)],
        out_specs=pl.BlockSpec((R * ROW, LANES), lambda i, lo, hi, nv, blk: (i, 0)),
    )
    return pl.pallas_call(
        functools.partial(_expert_kernel, R=R),
        grid_spec=grid_spec,
        out_shape=jax.ShapeDtypeStruct((n_tiles * R * ROW, LANES), F32),
        compiler_params=pltpu.CompilerParams(dimension_semantics=("arbitrary",),
                                             vmem_limit_bytes=VMEM_LIMIT),
        name="experts",
    )(tile_lo, tile_hi, tile_nv, tile_blk, xs, wg, wu, wd, wg, wu, wd)


def _return_kernel(slot_ref, slot_next_ref, ys_ref, x1_ref, fin_ref, y_ref, buf_ref, sem, *, tb):
    i = pl.program_id(0)
    cur = i % 2

    def copy_into(b):
        return lambda t, slot: _row_copy(ys_ref, slot, buf_ref.at[b], t, ROW, sem.at[b])

    @pl.when(i == 0)
    def _():
        _start_rows(tb, slot_ref, copy_into(0))

    @pl.when(i + 1 < pl.num_programs(0))
    def _():
        _start_rows(tb, slot_next_ref, copy_into(1 - cur))

    _wait_rows(tb, copy_into(cur))
    moe = jnp.concatenate([buf_ref[cur, pl.ds(c, tb, stride=ROW), :] for c in range(ROW)], axis=1)
    y_ref[...] = _rms(x1_ref[...] + moe, fin_ref[...])


def _return_call(slot, ys, x1, fin, tb):
    T, d = x1.shape
    n = T // tb
    return pl.pallas_call(
        functools.partial(_return_kernel, tb=tb),
        grid=(n,),
        in_specs=[pl.BlockSpec((tb,), lambda i: (i,), memory_space=pltpu.SMEM),
                  pl.BlockSpec((tb,), lambda i: (jnp.minimum(i + 1, n - 1),),
                               memory_space=pltpu.SMEM),
                  pl.BlockSpec(memory_space=pl.ANY),
                  pl.BlockSpec((tb, d), lambda i: (i, 0)),
                  pl.BlockSpec(fin.shape, lambda i: (0, 0))],
        out_specs=pl.BlockSpec((tb, d), lambda i: (i, 0)),
        out_shape=jax.ShapeDtypeStruct((T, d), F32),
        scratch_shapes=[pltpu.VMEM((2, tb * ROW, LANES), F32), pltpu.SemaphoreType.DMA((2,))],
        compiler_params=pltpu.CompilerParams(dimension_semantics=("arbitrary",),
                                             vmem_limit_bytes=VMEM_LIMIT),
        name="moe_return",
    )(slot, slot, ys, x1, fin)


def _routing_tables(counts, T, R):
    n_tiles = T // R + N_GROUPS * (EXPERTS_PER_GROUP * (EXPERTS_PER_GROUP - 1) // 2)
    cnt = counts.astype(jnp.int32)
    tiles_b = (cnt + R - 1) // R
    tile_end = jnp.cumsum(tiles_b)
    tile_start = tile_end - tiles_b
    off = tile_start * R
    total = tile_end[-1]
    i = jnp.arange(n_tiles, dtype=jnp.int32)
    blk = jnp.minimum(i, total - 1)
    b = jnp.searchsorted(tile_end, blk, side="right").astype(jnp.int32)
    nv = jnp.where(i < total, jnp.clip(cnt[b] - (blk - tile_start[b]) * R, 0, R), 0)
    grp = (b // PAIR_SLOTS) * EXPERTS_PER_GROUP
    lo = grp + (b % PAIR_SLOTS) // EXPERTS_PER_GROUP
    hi = grp + b % EXPERTS_PER_GROUP
    return cnt, off, total.reshape(1), lo, hi, nv.astype(jnp.int32), blk, n_tiles


def _rope_tables(S):
    inv = ROPE_THETA ** (-jnp.arange(0, QK_ROPE, 2, dtype=F32) / QK_ROPE)
    ang = jnp.arange(S, dtype=F32)[:, None] * inv[None, :]
    cos, sin = jnp.cos(ang), jnp.sin(ang)
    half = QK_ROPE // 2
    z = lambda n: jnp.zeros((S, n), F32)
    cos_t = jnp.concatenate([cos, cos, z(LANES - QK_ROPE)], axis=1)
    nsin_lo = jnp.concatenate([-sin, z(LANES - half)], axis=1)
    sin_hi = jnp.concatenate([z(half), sin, z(LANES - QK_ROPE)], axis=1)
    return cos_t, nsin_lo, sin_hi


def _prep_params(p):
    d = p["w_in"].shape[0]
    row = lambda a: a.reshape(1, -1).astype(F32)
    w_in = p["w_in"]
    c_q, c_kv, c_r = Q_LORA, Q_LORA + KV_LORA, Q_LORA + KV_LORA + QK_ROPE
    win = jnp.concatenate(
        [w_in[:, :c_r], jnp.zeros((d, LANES - QK_ROPE), F32), w_in[:, c_r:]], axis=1).astype(BF16)
    wqb = p["w_q_b"].reshape(Q_LORA, N_HEADS, QK_NOPE + QK_ROPE)
    wqb = jnp.pad(wqb, ((0, 0), (0, 0), (0, HEAD_PAD - QK_NOPE - QK_ROPE)))
    wqb = wqb.reshape(Q_LORA, N_HEADS * HEAD_PAD).astype(BF16)
    wkvb = p["w_kv_b"].reshape(KV_LORA, N_HEADS, QK_NOPE + V_HEAD)
    wkvb = jnp.concatenate([wkvb[:, :, :QK_NOPE].reshape(KV_LORA, -1),
                            wkvb[:, :, QK_NOPE:].reshape(KV_LORA, -1)], axis=1).astype(BF16)
    wlru = jnp.concatenate([p["lru_a_w"], p["lru_x_w"]], axis=-1).astype(BF16)
    wr = jnp.concatenate([p["router_expert_w"], p["router_group_w"],
                          jnp.zeros((d, LANES - N_EXPERTS - N_GROUPS), F32)], axis=1)
    wr_hi = wr.astype(BF16)
    wr = jnp.concatenate([wr_hi, (wr - wr_hi.astype(F32)).astype(BF16)], axis=1)
    br =jnp.concatenate([p["router_expert_b"], p["router_group_b"],
                          jnp.zeros((LANES - N_EXPERTS - N_GROUPS,), F32)]).reshape(1, LANES)
    return dict(
        mixg=row(p["mix_norm"]), win=win, qg=row(p["q_a_norm"]), wqb=wqb, kvg=row(p["kv_a_norm"]),
        wkvb=wkvb, gateb=p["gate_b"].astype(F32), cw=p["conv_w"].astype(F32), cb=row(p["conv_b"]),
        wlru=wlru, ab=p["lru_a_b"].astype(F32), xb=p["lru_x_b"].astype(F32),
        lam=p["lru_lambda"].astype(F32), woa=p["w_o_attn"].astype(BF16),
        wol=p["w_o_lru"].astype(BF16), wout=p["w_out"].astype(BF16), fg=row(p["ffn_norm"]),
        wr=wr, br=br, wg=p["w_gate_e"].astype(BF16), wu=p["w_up_e"].astype(BF16),
        wd=p["w_down_e"].astype(BF16), fin=row(p["final_norm"]))


def _trunk(x, w):
    B, S, d = x.shape
    T = B * S
    x2d = x.reshape(T, d)
    tm = min(512, S)
    tabs = _rope_tables(S)
    q, k, v, xc, gelu_g, gatt, grec = _proj_call(
        x2d, B, S, tabs, w["mixg"], w["win"], w["qg"], w["wqb"], w["kvg"], w["wkvb"], w["gateb"],
        w["cw"], w["cb"], tm)
    o = _attn_call(q, k, v, tq=min(1024, S), tk=min(1024, S), unroll=4)

    L = min(256, S)
    xc3 = xc.reshape(B, S, d)
    gelu3 = gelu_g.reshape(B, S, d)
    hb = _lru_call(xc3, w["wlru"][1], w["ab"][1:2], w["xb"][1:2], w["lam"][1:2], None, None, L,
                   reverse=True)
    yl = _lru_call(xc3, w["wlru"][0], w["ab"][0:1], w["xb"][0:1], w["lam"][0:1], hb, gelu3, L,
                   reverse=False)

    x1, pk, meta, cnt = _merge_call(x2d, o.reshape(T, d), yl.reshape(T, d), gatt, grec, w["woa"],
                                    w["wol"], w["wout"], w["fg"], w["wr"], w["br"], tm=min(512, S))
    R = 256
    cnt, off, total, lo, hi, nv, blk, n_tiles = _routing_tables(cnt[0], T, R)
    slot = off[meta[:, META_BUCKET]] + meta[:, META_RANK]
    xs = _dispatch_call(slot, off, cnt, total, pk, meta, n_tiles, R, tb=min(1024, T))
    ys = _expert_call(lo, hi, nv, blk, xs, w["wg"], w["wu"], w["wd"], R)
    y = _return_call(slot, ys, x1, w["fin"], tb=min(1024, T))
    return y.reshape(B, S, d)


def kernel(x_prompt, x_sample, mix_norm, w_in, q_a_norm, w_q_b, kv_a_norm, w_kv_b, w_o_attn, conv_w, conv_b, lru_a_w, lru_a_b, lru_x_w, lru_x_b, lru_lambda, w_o_lru, gate_b, w_out, ffn_norm, router_group_w, router_group_b, router_expert_w, router_expert_b, w_gate_e, w_up_e, w_down_e, final_norm):
    depth = mix_norm.shape[0]
    layers = []
    for l in range(depth):
        layers.append(_prep_params(dict(
            mix_norm=mix_norm[l], w_in=w_in[l], q_a_norm=q_a_norm[l], w_q_b=w_q_b[l],
            kv_a_norm=kv_a_norm[l], w_kv_b=w_kv_b[l], w_o_attn=w_o_attn[l], conv_w=conv_w[l],
            conv_b=conv_b[l], lru_a_w=lru_a_w[l], lru_a_b=lru_a_b[l], lru_x_w=lru_x_w[l],
            lru_x_b=lru_x_b[l], lru_lambda=lru_lambda[l], w_o_lru=w_o_lru[l], gate_b=gate_b[l],
            w_out=w_out[l], ffn_norm=ffn_norm[l], router_group_w=router_group_w[l],
            router_group_b=router_group_b[l], router_expert_w=router_expert_w[l],
            router_expert_b=router_expert_b[l], w_gate_e=w_gate_e[l], w_up_e=w_up_e[l],
            w_down_e=w_down_e[l], final_norm=final_norm)))
    assert depth == 1

    def trunk(x):
        for w in layers:
            x = _trunk(x, w)
        return x

    return (trunk(x_prompt), trunk(x_sample))
```

```python
import functools
import math

import jax
import jax.numpy as jnp
from jax import lax
from jax.experimental import pallas as pl
from jax.experimental.pallas import tpu as pltpu

N_HEADS = 8
QK_NOPE = 128
QK_ROPE = 64
V_HEAD = 128
Q_LORA = 384
KV_LORA = 256
ROPE_THETA = 10000.0
LRU_BLOCKS = 8
LRU_C = 8.0
N_GROUPS = 4
EXPERTS_PER_GROUP = 8
N_EXPERTS = N_GROUPS * EXPERTS_PER_GROUP
EPS = 1e-6

LANES = 128
HEAD_PAD = 256
LAT_COLS = Q_LORA + KV_LORA + LANES
PAIR_SLOTS = EXPERTS_PER_GROUP * EXPERTS_PER_GROUP
N_BUCKETS = N_GROUPS * PAIR_SLOTS
ROW = 8
ROW_PK = 4
META_BUCKET, META_RANK, META_W_LO, META_W_HI = 0, 1, 2, 3
HI16 = -65536
DMA_GROUP = 16
SQRT_FLOOR = 1e-36
VMEM_LIMIT = 56 * 1024 * 1024

F32 = jnp.float32
BF16 = jnp.bfloat16


def _sigmoid(z):
    return 1.0 / (1.0 + jnp.exp2(z * (-math.log2(math.e))))


def _rms(x, g):
    var = jnp.mean(x * x, axis=-1, keepdims=True)
    return x * lax.rsqrt(var + EPS) * g


def _rope(pe, cos_t, nsin_lo, sin_hi):
    x2_to_lo = pltpu.roll(pe, 96, axis=1)
    x1_to_hi = pltpu.roll(pe, 32, axis=1)
    return pe * cos_t + x2_to_lo * nsin_lo + x1_to_hi * sin_hi


def _conv_time(x, prev, nxt, cw_ref, cb_ref):
    L, W = x.shape
    row8 = lax.broadcasted_iota(jnp.int32, (8, W), 0)

    def shifted(k, head, tail):
        r = pltpu.roll(x, k % L, axis=0)
        return jnp.concatenate([head(r[0:8]), r[8:L - 8], tail(r[L - 8:L])], axis=0)

    keep = lambda s: s
    xm1 = shifted(1, lambda s: jnp.where(row8 == 0, prev[7:8, :], s), keep)
    xm2 = shifted(2, lambda s: jnp.where(row8 == 0, prev[6:7, :],
                                         jnp.where(row8 == 1, prev[7:8, :], s)), keep)
    xp1 = shifted(-1, keep, lambda s: jnp.where(row8 == 7, nxt[0:1, :], s))
    return cb_ref[...] + (xm2 * cw_ref[0:1, :] + xm1 * cw_ref[1:2, :] + x * cw_ref[2:3, :]
                          + xp1 * cw_ref[3:4, :])


def _proj_kernel(x_ref, xp_ref, xn_ref, cos_ref, nsl_ref, sh_ref, mixg_ref, win_ref, qg_ref,
                 wqb_ref, kvg_ref, wkvb_ref, gateb_ref, cw_ref, cb_ref, q_ref, k_ref, v_ref,
                 xc_ref, gelu_ref, gatt_ref, grec_ref, *, scale, nt_seq):
    d = x_ref.shape[-1]
    xn = _rms(x_ref[...], mixg_ref[...]).astype(BF16)
    lat = jnp.dot(xn, win_ref[:, 0:LAT_COLS], preferred_element_type=F32)
    cos_t, nsl, sh = cos_ref[...], nsl_ref[...], sh_ref[...]

    qn = _rms(lat[:, 0:Q_LORA], qg_ref[...]).astype(BF16)
    qf = jnp.dot(qn, wqb_ref[...], preferred_element_type=F32)
    for h in range(N_HEADS):
        base = h * HEAD_PAD
        q_ref[0, h, :, 0:LANES] = (qf[:, base:base + LANES] * scale).astype(BF16)
        pe = _rope(qf[:, base + LANES:base + HEAD_PAD], cos_t, nsl, sh)
        q_ref[0, h, :, LANES:HEAD_PAD] = (pe * scale).astype(BF16)

    kvn = _rms(lat[:, Q_LORA:Q_LORA + KV_LORA], kvg_ref[...]).astype(BF16)
    kvf = jnp.dot(kvn, wkvb_ref[...], preferred_element_type=F32)
    kpe = _rope(lat[:, Q_LORA + KV_LORA:LAT_COLS], cos_t, nsl, sh).astype(BF16)
    for h in range(N_HEADS):
        k_ref[0, h, :, 0:LANES] = kvf[:, h * LANES:(h + 1) * LANES].astype(BF16)
        k_ref[0, h, :, LANES:HEAD_PAD] = kpe
        v_ref[0, h, :, :] = kvf[:, (N_HEADS + h) * LANES:(N_HEADS + h + 1) * LANES].astype(BF16)

    c0 = LAT_COLS
    ti = pl.program_id(0) % nt_seq
    halo = jnp.concatenate([xp_ref[...], xn_ref[...]], axis=0)
    halo = jnp.dot(_rms(halo, mixg_ref[...]).astype(BF16), win_ref[:, c0:c0 + d],
                   preferred_element_type=F32)
    prev = jnp.where(ti > 0, halo[0:8], 0.0)
    nxt = jnp.where(ti < nt_seq - 1, halo[8:16], 0.0)
    xl = jnp.dot(xn, win_ref[:, c0:c0 + d], preferred_element_type=F32)
    xc_ref[...] = _conv_time(xl, prev, nxt, cw_ref, cb_ref)
    g = jnp.dot(xn, win_ref[:, c0 + d:c0 + 2 * d], preferred_element_type=F32)
    gelu_ref[...] = jax.nn.gelu(g).astype(BF16)
    ga = jnp.dot(xn, win_ref[:, c0 + 2 * d:c0 + 3 * d], preferred_element_type=F32)
    gatt_ref[...] = _sigmoid(ga + gateb_ref[0:1, :]).astype(BF16)
    gb = jnp.dot(xn, win_ref[:, c0 + 3 * d:c0 + 4 * d], preferred_element_type=F32)
    grec_ref[...] = _sigmoid(gb + gateb_ref[1:2, :]).astype(BF16)


def _proj_call(x2d, B, S, tabs, mixg, win, qg, wqb, kvg, wkvb, gateb, cw, cb, tm):
    T, d = x2d.shape
    nt_seq = S // tm
    r8 = tm // 8
    halo_prev = pl.BlockSpec((8, d), lambda i: (jnp.maximum(i * r8 - 1, 0), 0))
    halo_next = pl.BlockSpec((8, d), lambda i: (jnp.minimum((i + 1) * r8, T // 8 - 1), 0))
    full = lambda a: pl.BlockSpec(a.shape, lambda i: (0,) * a.ndim, pipeline_mode=pl.Buffered(1))
    tab_spec = pl.BlockSpec((tm, LANES), lambda i: (i % nt_seq, 0))
    tok = lambda w: pl.BlockSpec((tm, w), lambda i: (i, 0))
    head_spec = lambda w: pl.BlockSpec((1, N_HEADS, tm, w), lambda i: (i // nt_seq, 0, i % nt_seq, 0))
    scale = math.log2(math.e) / math.sqrt(QK_NOPE + QK_ROPE)
    return pl.pallas_call(
        functools.partial(_proj_kernel, scale=scale, nt_seq=nt_seq),
        grid=(T // tm,),
        in_specs=[tok(d), halo_prev, halo_next, tab_spec, tab_spec, tab_spec, full(mixg), full(win),
                  full(qg), full(wqb), full(kvg), full(wkvb), full(gateb), full(cw), full(cb)],
        out_specs=[head_spec(HEAD_PAD), head_spec(HEAD_PAD), head_spec(V_HEAD),
                   tok(d), tok(d), tok(d), tok(d)],
        out_shape=[
            jax.ShapeDtypeStruct((B, N_HEADS, S, HEAD_PAD), BF16),
            jax.ShapeDtypeStruct((B, N_HEADS, S, HEAD_PAD), BF16),
            jax.ShapeDtypeStruct((B, N_HEADS, S, V_HEAD), BF16),
            jax.ShapeDtypeStruct((T, d), F32),
            jax.ShapeDtypeStruct((T, d), BF16),
            jax.ShapeDtypeStruct((T, d), BF16),
            jax.ShapeDtypeStruct((T, d), BF16),
        ],
        compiler_params=pltpu.CompilerParams(dimension_semantics=("arbitrary",),
                                             vmem_limit_bytes=VMEM_LIMIT),
        name="proj",
    )(x2d, x2d, x2d, *tabs, mixg, win, qg, wqb, kvg, wkvb, gateb, cw, cb)


def _attn_kernel(q_ref, k_ref, v_ref, o_ref, s_ref, m_ref, acc_ref, *, tk, unroll):
    q = q_ref[0, 0]
    nk = k_ref.shape[2] // tk
    n_rep = tk // LANES

    def scores(j):
        off = j * tk if isinstance(j, int) else pl.multiple_of(j * tk, tk)
        kc = k_ref[0, 0, pl.ds(off, tk), :]
        return lax.dot_general(q, kc, (((1,), (1,)), ((), ())), preferred_element_type=F32)

    ones_col = (lax.broadcasted_iota(jnp.int32, (tk, LANES), 1) == 0).astype(BF16)

    def consume(j, s):
        off = j * tk if isinstance(j, int) else pl.multiple_of(j * tk, tk)
        m_old = m_ref[...]
        m_new = jnp.maximum(m_old, jnp.max(s, axis=-1, keepdims=True))
        alpha = jnp.exp2(m_old - m_new)
        p = jnp.exp2(s - jnp.tile(m_new, (1, n_rep)))
        v1 = jnp.concatenate([v_ref[0, 0, pl.ds(off, tk), :], ones_col], axis=1)
        pv = jnp.dot(p.astype(BF16), v1, preferred_element_type=F32)
        acc_ref[...] = jnp.tile(alpha, (1, 2)) * acc_ref[...] + pv
        m_ref[...] = m_new

    m_ref[...] = jnp.full(m_ref.shape, -jnp.inf, F32)
    acc_ref[...] = jnp.zeros(acc_ref.shape, F32)
    s_ref[0] = scores(0)

    def step(j, cur):
        s = s_ref[cur]
        s_ref[1 - cur] = scores(j + 1)
        consume(j, s)

    if nk <= unroll:
        for j in range(nk - 1):
            step(j, j % 2)
    else:
        assert unroll % 2 == 0 and nk % unroll == 0
        def group(g, _):
            for u in range(unroll):
                step(unroll * g + u, u % 2)
            return 0
        lax.fori_loop(0, nk // unroll - 1, group, 0)
        for j in range(nk - unroll, nk - 1):
            step(j, j % 2)
    consume(nk - 1, s_ref[(nk - 1) % 2])
    l = jnp.sum(acc_ref[:, V_HEAD:2 * V_HEAD], axis=-1, keepdims=True)
    o_ref[0] = (acc_ref[:, 0:V_HEAD] / l).astype(BF16)


def _attn_call(q, k, v, tq, tk, unroll=8):
    B, H, S, _ = q.shape
    return pl.pallas_call(
        functools.partial(_attn_kernel, tk=tk, unroll=unroll),
        grid=(B, H, S // tq),
        in_specs=[
            pl.BlockSpec((1, 1, tq, HEAD_PAD), lambda b, h, i: (b, h, i, 0)),
            pl.BlockSpec((1, 1, S, HEAD_PAD), lambda b, h, i: (b, h, 0, 0)),
            pl.BlockSpec((1, 1, S, V_HEAD), lambda b, h, i: (b, h, 0, 0)),
        ],
        out_specs=pl.BlockSpec((1, tq, V_HEAD), lambda b, h, i: (b, i, h)),
        out_shape=jax.ShapeDtypeStruct((B, S, H * V_HEAD), BF16),
        scratch_shapes=[pltpu.VMEM((2, tq, tk), F32), pltpu.VMEM((tq, LANES), F32),
                        pltpu.VMEM((tq, 2 * V_HEAD), F32)],
        compiler_params=pltpu.CompilerParams(
            dimension_semantics=("arbitrary", "arbitrary", "arbitrary"),
            vmem_limit_bytes=VMEM_LIMIT),
        name="attn",
    )(q, k, v)


def _lru_kernel(*refs, reverse):
    if reverse:
        (x_ref, w_ref, ab_ref, xb_ref, lam_ref, out_ref, h_ref, a_s, b_s, h_s) = refs
    else:
        (x_ref, w_ref, ab_ref, xb_ref, lam_ref, hb_ref, gelu_ref,
         out_ref, h_ref, a_s, b_s, h_s) = refs
    L, W = x_ref.shape[1], x_ref.shape[2]
    blk = W // LRU_BLOCKS

    @pl.when(pl.program_id(1) == 0)
    def _():
        h_ref[...] = jnp.zeros(h_ref.shape, F32)

    xc = x_ref[0]
    xcb = xc.astype(BF16)
    sp = jnp.log1p(jnp.exp(-jnp.abs(lam_ref[...]))) + jnp.maximum(-lam_ref[...], 0.0)
    sp2 = sp * (-LRU_C * math.log2(math.e))
    for n in range(LRU_BLOCKS):
        sl = slice(n * blk, (n + 1) * blk)
        z = jnp.dot(xcb[:, sl], w_ref[n], preferred_element_type=F32)
        r = _sigmoid(z[:, 0:blk] + ab_ref[:, sl])
        i = _sigmoid(z[:, blk:2 * blk] + xb_ref[:, sl])
        a = jnp.exp2(r * sp2[:, sl])
        a_s[:, sl] = a
        om = 1.0 - a * a
        root = om * lax.rsqrt(jnp.maximum(om, SQRT_FLOOR))
        b_s[:, sl] = root * (i * xc[:, sl])

    def step(t, h):
        tt = (L - 1 - t) if reverse else t
        h = a_s[pl.ds(tt, 1), :] * h + b_s[pl.ds(tt, 1), :]
        h_s[pl.ds(tt, 1), :] = h
        return h

    h_ref[...] = lax.fori_loop(0, L, step, h_ref[...], unroll=8)
    if reverse:
        out_ref[0] = h_s[...].astype(BF16)
    else:
        hsum = h_s[...] + hb_ref[0].astype(F32)
        out_ref[0] = (hsum * gelu_ref[0].astype(F32)).astype(BF16)


def _lru_call(xc, w, ab, xb, lam, hb, gelu, L, reverse):
    B, S, W = xc.shape
    nC = S // L
    pos = (lambda j: nC - 1 - j) if reverse else (lambda j: j)
    full = lambda a: pl.BlockSpec(a.shape, lambda b, j: (0,) * a.ndim)
    chunk = pl.BlockSpec((1, L, W), lambda b, j: (b, pos(j), 0))
    in_specs = [chunk, full(w), full(ab), full(xb), full(lam)]
    args = [xc, w, ab, xb, lam]
    if not reverse:
        in_specs += [chunk, chunk]
        args += [hb, gelu]
    return pl.pallas_call(
        functools.partial(_lru_kernel, reverse=reverse),
        grid=(B, nC),
        in_specs=in_specs,
        out_specs=chunk,
        out_shape=jax.ShapeDtypeStruct((B, S, W), BF16),
        scratch_shapes=[pltpu.VMEM((1, W), F32), pltpu.VMEM((L, W), F32), pltpu.VMEM((L, W), F32),
                        pltpu.VMEM((L, W), F32)],
        compiler_params=pltpu.CompilerParams(dimension_semantics=("arbitrary", "arbitrary"),
                                             vmem_limit_bytes=VMEM_LIMIT),
        name="lru_bwd" if reverse else "lru_fwd",
    )(*args)


def _merge_kernel(x_ref, o_ref, yl_ref, gatt_ref, grec_ref, woa_ref, wol_ref, wout_ref, fg_ref,
                  wr_ref, br_ref, x1_ref, pk_ref, meta_ref, cnt_ref):
    tm = x_ref.shape[0]
    y_attn = jnp.dot(o_ref[...], woa_ref[...], preferred_element_type=F32)
    y_lru = jnp.dot(yl_ref[...], wol_ref[...], preferred_element_type=F32)
    mix = gatt_ref[...].astype(F32) * y_attn + grec_ref[...].astype(F32) * y_lru
    x1 = x_ref[...] + jnp.dot(mix.astype(BF16), wout_ref[...], preferred_element_type=F32)
    xn2 = _rms(x1, fg_ref[...])

    x_hi = xn2.astype(BF16)
    x_lo = (xn2 - x_hi.astype(F32)).astype(BF16)
    hh = jnp.dot(x_hi, wr_ref[...], preferred_element_type=F32)
    lh = jnp.dot(x_lo, wr_ref[:, 0:LANES], preferred_element_type=F32)
    logits = hh[:, 0:LANES] + (hh[:, LANES:2 * LANES] + lh) + br_ref[...]
    lane = lax.broadcasted_iota(jnp.int32, logits.shape, 1)
    big = jnp.int32(LANES)
    neg = jnp.float32(-jnp.inf)
    is_g = (lane >= N_EXPERTS) & (lane < N_EXPERTS + N_GROUPS)
    gl = jnp.where(is_g, logits, neg)
    gmax = jnp.max(gl, axis=-1, keepdims=True)
    g_p = 1.0 / jnp.sum(jnp.exp(gl - gmax), axis=-1, keepdims=True)
    g_idx = jnp.min(jnp.where(gl == gmax, lane - N_EXPERTS, big), axis=-1, keepdims=True)
    in_grp = (lane < N_EXPERTS) & ((lane // EXPERTS_PER_GROUP) == g_idx)
    el = jnp.where(in_grp, logits, neg)
    m1 = jnp.max(el, axis=-1, keepdims=True)
    i1 = jnp.min(jnp.where(el == m1, lane, big), axis=-1, keepdims=True)
    el2 = jnp.where(lane == i1, neg, el)
    m2 = jnp.max(el2, axis=-1, keepdims=True)
    i2 = jnp.min(jnp.where(el2 == m2, lane, big), axis=-1, keepdims=True)
    e2 = jnp.exp(m2 - m1)
    w1 = g_p / (1.0 + e2)
    w2 = w1 * e2

    first_lo = i1 < i2
    e_lo = jnp.where(first_lo, i1, i2)
    e_hi = jnp.where(first_lo, i2, i1)
    w_lo = jnp.where(first_lo, w1, w2)
    w_hi = jnp.where(first_lo, w2, w1)
    bucket = g_idx * PAIR_SLOTS + (e_lo % EXPERTS_PER_GROUP) * EXPERTS_PER_GROUP + e_hi % EXPERTS_PER_GROUP

    @pl.when(pl.program_id(0) == 0)
    def _():
        cnt_ref[...] = jnp.zeros(cnt_ref.shape, F32)

    blane = lax.broadcasted_iota(jnp.int32, (tm, N_BUCKETS), 1)
    onehot = blane == bucket
    tri = (lax.broadcasted_iota(jnp.int32, (tm, tm), 0)
           > lax.broadcasted_iota(jnp.int32, (tm, tm), 1))
    earlier = jnp.dot(tri.astype(BF16), onehot.astype(BF16), preferred_element_type=F32)
    base = cnt_ref[0:1, :]
    rank = jnp.sum(jnp.where(onehot, earlier + base, 0.0), axis=-1, keepdims=True)
    cnt_ref[0:1, :] = base + jnp.sum(onehot.astype(F32), axis=0, keepdims=True)
    bits = lambda v: lax.bitcast_convert_type(v, jnp.int32)
    meta_ref[...] = jnp.where(
        lane == META_BUCKET, bucket,
        jnp.where(lane == META_RANK, rank.astype(jnp.int32),
                  jnp.where(lane == META_W_LO, bits(w_lo),
                            jnp.where(lane == META_W_HI, bits(w_hi), 0))))

    x1_ref[...] = x1
    half = x1.shape[1] // 2
    xb = x_hi.astype(F32)
    pk_ref[...] = (bits(xb[:, half:]) & HI16) | lax.shift_right_logical(bits(xb[:, :half]), 16)


def _merge_call(x2d, o, yl, gatt, grec, woa, wol, wout, fg, wr, br, tm):
    T, d = x2d.shape
    assert d == 2 * ROW_PK * LANES
    full = lambda a: pl.BlockSpec(a.shape, lambda i: (0,) * a.ndim)
    tok = lambda w: pl.BlockSpec((tm, w), lambda i: (i, 0))
    return pl.pallas_call(
        _merge_kernel,
        grid=(T // tm,),
        in_specs=[tok(d), tok(d), tok(d), tok(d), tok(d), full(woa), full(wol), full(wout),
                  full(fg), full(wr), full(br)],
        out_specs=[tok(d), tok(d // 2), tok(LANES),
                   pl.BlockSpec((8, N_BUCKETS), lambda i: (0, 0))],
        out_shape=[jax.ShapeDtypeStruct((T, d), F32),
                   jax.ShapeDtypeStruct((T, d // 2), jnp.int32),
                   jax.ShapeDtypeStruct((T, LANES), jnp.int32),
                   jax.ShapeDtypeStruct((8, N_BUCKETS), F32)],
        compiler_params=pltpu.CompilerParams(dimension_semantics=("arbitrary",),
                                             vmem_limit_bytes=VMEM_LIMIT),
        name="merge",
    )(x2d, o, yl, gatt, grec, woa, wol, wout, fg, wr, br)


def _row_copy(src, src_row, dst, dst_row, rows, sem):
    return pltpu.make_async_copy(src.at[pl.ds(src_row * rows, rows)],
                                 dst.at[pl.ds(dst_row * rows, rows)], sem)


def _start_rows(n_rows, slot_ref, copy_of):
    def start(g, _):
        rows = [g * DMA_GROUP + u for u in range(DMA_GROUP)]
        slots = [slot_ref[t] for t in rows]
        for t, s in zip(rows, slots):
            copy_of(t, s).start()
        return 0

    lax.fori_loop(0, n_rows // DMA_GROUP, start, 0)


def _wait_rows(n_rows, copy_of):
    def wait(g, _):
        for u in range(DMA_GROUP):
            copy_of(g * DMA_GROUP + u, 0).wait()
        return 0

    lax.fori_loop(0, n_rows // DMA_GROUP, wait, 0)


def _fill_unused_slots(cnt_ref, off_ref, total_ref, zero_ref, xs_ref, sem, R, n_tiles):
    zero_ref[...] = jnp.zeros(zero_ref.shape, zero_ref.dtype)
    bits = R.bit_length() - 1

    def run_copy(pos, size):
        return pltpu.make_async_copy(zero_ref.at[pl.ds(0, size * ROW)],
                                     xs_ref.at[pl.ds(pos * ROW, size * ROW)], sem)

    def bucket_pads(b, op):
        pad = (-cnt_ref[b]) & (R - 1)
        pos = off_ref[b] + cnt_ref[b]
        for k in range(bits):
            has = (pad >> k) & 1

            @pl.when(has == 1)
            def _():
                op(run_copy(pos, 1 << k))

            pos = pos + has * (1 << k)

    def tail_tile(i, op):
        op(run_copy(i * R, R))

    for op in (lambda c: c.start(), lambda c: c.wait()):
        lax.fori_loop(0, N_BUCKETS, lambda b, _: (bucket_pads(b, op), 0)[1], 0)
        lax.fori_loop(total_ref[0], n_tiles, lambda i, _: (tail_tile(i, op), 0)[1], 0)


def _dispatch_kernel(slot_ref, off_ref, cnt_ref, total_ref, pk_ref, meta_ref, xs_ref,
                     stage_ref, zero_ref, sem, fill_sem, *, tb, R, n_tiles):
    @pl.when(pl.program_id(0) == 0)
    def _():
        stage_ref[...] = jnp.zeros(stage_ref.shape, stage_ref.dtype)
        _fill_unused_slots(cnt_ref, off_ref, total_ref, zero_ref, xs_ref, fill_sem, R, n_tiles)

    for c in range(ROW_PK):
        stage_ref[pl.ds(c, tb, stride=ROW), :] = pk_ref[:, c * LANES:(c + 1) * LANES]
    stage_ref[pl.ds(ROW_PK, tb, stride=ROW), :] = meta_ref[...]
    copy = lambda t, slot: _row_copy(stage_ref, t, xs_ref, slot, ROW, sem)
    _start_rows(tb, slot_ref, copy)
    _wait_rows(tb, copy)


def _dispatch_call(slot, off, cnt, total, pk, meta, n_tiles, R, tb):
    T = slot.shape[0]
    assert pk.shape[1] == ROW_PK * LANES
    smem_blk = pl.BlockSpec((tb,), lambda i: (i,), memory_space=pltpu.SMEM)
    smem = pl.BlockSpec(memory_space=pltpu.SMEM)
    return pl.pallas_call(
        functools.partial(_dispatch_kernel, tb=tb, R=R, n_tiles=n_tiles),
        grid=(T // tb,),
        in_specs=[smem_blk, smem, smem, smem,
                  pl.BlockSpec((tb, ROW_PK * LANES), lambda i: (i, 0)),
                  pl.BlockSpec((tb, LANES), lambda i: (i, 0))],
        out_specs=pl.BlockSpec(memory_space=pl.ANY),
        out_shape=jax.ShapeDtypeStruct((n_tiles * R * ROW, LANES), jnp.int32),
        scratch_shapes=[pltpu.VMEM((tb * ROW, LANES), jnp.int32),
                        pltpu.VMEM((R * ROW, LANES), jnp.int32),
                        pltpu.SemaphoreType.DMA(()), pltpu.SemaphoreType.DMA(())],
        compiler_params=pltpu.CompilerParams(dimension_semantics=("arbitrary",),
                                             vmem_limit_bytes=VMEM_LIMIT),
        name="dispatch",
    )(slot, off, cnt, total, pk, meta)


def _expert_kernel(lo_ref, hi_ref, nv_ref, blk_ref, xs_ref, wgl_ref, wul_ref, wdl_ref,
                   wgh_ref, wuh_ref, wdh_ref, ys_ref, *, R):
    nv = nv_ref[pl.program_id(0)]

    @pl.when(nv == 0)
    def _():
        ys_ref[...] = jnp.zeros(ys_ref.shape, F32)

    @pl.when(nv > 0)
    def _():
        f32 = lambda v: lax.bitcast_convert_type(v, F32)
        words = [xs_ref[pl.ds(c, R, stride=ROW), :] for c in range(ROW_PK)]
        xn = jnp.concatenate([f32(w << 16) for w in words] + [f32(w & HI16) for w in words],
                             axis=1).astype(BF16)
        meta = f32(xs_ref[pl.ds(ROW_PK, R, stride=ROW), :])
        lane = lax.broadcasted_iota(jnp.int32, meta.shape, 1)
        w_lo = jnp.sum(jnp.where(lane == META_W_LO, meta, 0.0), axis=-1, keepdims=True)
        w_hi = jnp.sum(jnp.where(lane == META_W_HI, meta, 0.0), axis=-1, keepdims=True)

        def hidden(wg_ref, wu_ref, w):
            gt = jnp.dot(xn, wg_ref[0], preferred_element_type=F32)
            up = jnp.dot(xn, wu_ref[0], preferred_element_type=F32)
            return ((gt * _sigmoid(gt)) * up * w).astype(BF16)

        moe = (jnp.dot(hidden(wgl_ref, wul_ref, w_lo), wdl_ref[0], preferred_element_type=F32)
               + jnp.dot(hidden(wgh_ref, wuh_ref, w_hi), wdh_ref[0], preferred_element_type=F32))
        for c in range(ROW):
            ys_ref[pl.ds(c, R, stride=ROW), :] = moe[:, c * LANES:(c + 1) * LANES]


def _expert_call(tile_lo, tile_hi, tile_nv, tile_blk, xs, wg, wu, wd, R):
    n_tiles = tile_lo.shape[0]
    E, d, de = wg.shape
    assert d == ROW * LANES
    lo_map = lambda i, lo, hi, nv, blk: (lo[i], 0, 0)
    hi_map = lambda i, lo, hi, nv, blk: (hi[i], 0, 0)
    grid_spec = pltpu.PrefetchScalarGridSpec(
        num_scalar_prefetch=4,
        grid=(n_tiles,),
        in_specs=[pl.BlockSpec((R * ROW, LANES), lambda i, lo, hi, nv, blk: (blk[i], 0)),
                  pl.BlockSpec((1, d, de), lo_map), pl.BlockSpec((1, d, de), lo_map),
                  pl.BlockSpec((1, de, d), lo_map),
                  pl.BlockSpec((1, d, de), hi_map), pl.BlockSpec((1, d, de), hi_map),
                  pl.BlockSpec((1, de, d), hi_map)],
        out_specs=pl.BlockSpec((R * ROW, LANES), lambda i, lo, hi, nv, blk: (i, 0)),
    )
    return pl.pallas_call(
        functools.partial(_expert_kernel, R=R),
        grid_spec=grid_spec,
        out_shape=jax.ShapeDtypeStruct((n_tiles * R * ROW, LANES), F32),
        compiler_params=pltpu.CompilerParams(dimension_semantics=("arbitrary",),
                                             vmem_limit_bytes=VMEM_LIMIT),
        name="experts",
    )(tile_lo, tile_hi, tile_nv, tile_blk, xs, wg, wu, wd, wg, wu, wd)


def _return_kernel(slot_ref, slot_next_ref, ys_ref, x1_ref, fin_ref, y_ref, buf_ref, sem, *, tb):
    i = pl.program_id(0)
    cur = i % 2

    def copy_into(b):
        return lambda t, slot: _row_copy(ys_ref, slot, buf_ref.at[b], t, ROW, sem.at[b])

    @pl.when(i == 0)
    def _():
        _start_rows(tb, slot_ref, copy_into(0))

    @pl.when(i + 1 < pl.num_programs(0))
    def _():
        _start_rows(tb, slot_next_ref, copy_into(1 - cur))

    _wait_rows(tb, copy_into(cur))
    moe = jnp.concatenate([buf_ref[cur, pl.ds(c, tb, stride=ROW), :] for c in range(ROW)], axis=1)
    y_ref[...] = _rms(x1_ref[...] + moe, fin_ref[...])


def _return_call(slot, ys, x1, fin, tb):
    T, d = x1.shape
    n = T // tb
    return pl.pallas_call(
        functools.partial(_return_kernel, tb=tb),
        grid=(n,),
        in_specs=[pl.BlockSpec((tb,), lambda i: (i,), memory_space=pltpu.SMEM),
                  pl.BlockSpec((tb,), lambda i: (jnp.minimum(i + 1, n - 1),),
                               memory_space=pltpu.SMEM),
                  pl.BlockSpec(memory_space=pl.ANY),
                  pl.BlockSpec((tb, d), lambda i: (i, 0)),
                  pl.BlockSpec(fin.shape, lambda i: (0, 0))],
        out_specs=pl.BlockSpec((tb, d), lambda i: (i, 0)),
        out_shape=jax.ShapeDtypeStruct((T, d), F32),
        scratch_shapes=[pltpu.VMEM((2, tb * ROW, LANES), F32), pltpu.SemaphoreType.DMA((2,))],
        compiler_params=pltpu.CompilerParams(dimension_semantics=("arbitrary",),
                                             vmem_limit_bytes=VMEM_LIMIT),
        name="moe_return",
    )(slot, slot, ys, x1, fin)


def _routing_tables(counts, T, R):
    n_tiles = T // R + N_GROUPS * (EXPERTS_PER_GROUP * (EXPERTS_PER_GROUP - 1) // 2)
    cnt = counts.astype(jnp.int32)
    tiles_b = (cnt + R - 1) // R
    tile_end = jnp.cumsum(tiles_b)
    tile_start = tile_end - tiles_b
    off = tile_start * R
    total = tile_end[-1]
    i = jnp.arange(n_tiles, dtype=jnp.int32)
    blk = jnp.minimum(i, total - 1)
    b = jnp.searchsorted(tile_end, blk, side="right").astype(jnp.int32)
    nv = jnp.where(i < total, jnp.clip(cnt[b] - (blk - tile_start[b]) * R, 0, R), 0)
    grp = (b // PAIR_SLOTS) * EXPERTS_PER_GROUP
    lo = grp + (b % PAIR_SLOTS) // EXPERTS_PER_GROUP
    hi = grp + b % EXPERTS_PER_GROUP
    return cnt, off, total.reshape(1), lo, hi, nv.astype(jnp.int32), blk, n_tiles


def _rope_tables(S):
    inv = ROPE_THETA ** (-jnp.arange(0, QK_ROPE, 2, dtype=F32) / QK_ROPE)
    ang = jnp.arange(S, dtype=F32)[:, None] * inv[None, :]
    cos, sin = jnp.cos(ang), jnp.sin(ang)
    half = QK_ROPE // 2
    z = lambda n: jnp.zeros((S, n), F32)
    cos_t = jnp.concatenate([cos, cos, z(LANES - QK_ROPE)], axis=1)
    nsin_lo = jnp.concatenate([-sin, z(LANES - half)], axis=1)
    sin_hi = jnp.concatenate([z(half), sin, z(LANES - QK_ROPE)], axis=1)
    return cos_t, nsin_lo, sin_hi


def _prep_params(p):
    d = p["w_in"].shape[0]
    row = lambda a: a.reshape(1, -1).astype(F32)
    w_in = p["w_in"]
    c_q, c_kv, c_r = Q_LORA, Q_LORA + KV_LORA, Q_LORA + KV_LORA + QK_ROPE
    win = jnp.concatenate(
        [w_in[:, :c_r], jnp.zeros((d, LANES - QK_ROPE), F32), w_in[:, c_r:]], axis=1).astype(BF16)
    wqb = p["w_q_b"].reshape(Q_LORA, N_HEADS, QK_NOPE + QK_ROPE)
    wqb = jnp.pad(wqb, ((0, 0), (0, 0), (0, HEAD_PAD - QK_NOPE - QK_ROPE)))
    wqb = wqb.reshape(Q_LORA, N_HEADS * HEAD_PAD).astype(BF16)
    wkvb = p["w_kv_b"].reshape(KV_LORA, N_HEADS, QK_NOPE + V_HEAD)
    wkvb = jnp.concatenate([wkvb[:, :, :QK_NOPE].reshape(KV_LORA, -1),
                            wkvb[:, :, QK_NOPE:].reshape(KV_LORA, -1)], axis=1).astype(BF16)
    wlru = jnp.concatenate([p["lru_a_w"], p["lru_x_w"]], axis=-1).astype(BF16)
    wr = jnp.concatenate([p["router_expert_w"], p["router_group_w"],
                          jnp.zeros((d, LANES - N_EXPERTS - N_GROUPS), F32)], axis=1)
    wr_hi = wr.astype(BF16)
    wr = jnp.concatenate([wr_hi, (wr - wr_hi.astype(F32)).astype(BF16)], axis=1)
    br =jnp.concatenate([p["router_expert_b"], p["router_group_b"],
                          jnp.zeros((LANES - N_EXPERTS - N_GROUPS,), F32)]).reshape(1, LANES)
    return dict(
        mixg=row(p["mix_norm"]), win=win, qg=row(p["q_a_norm"]), wqb=wqb, kvg=row(p["kv_a_norm"]),
        wkvb=wkvb, gateb=p["gate_b"].astype(F32), cw=p["conv_w"].astype(F32), cb=row(p["conv_b"]),
        wlru=wlru, ab=p["lru_a_b"].astype(F32), xb=p["lru_x_b"].astype(F32),
        lam=p["lru_lambda"].astype(F32), woa=p["w_o_attn"].astype(BF16),
        wol=p["w_o_lru"].astype(BF16), wout=p["w_out"].astype(BF16), fg=row(p["ffn_norm"]),
        wr=wr, br=br, wg=p["w_gate_e"].astype(BF16), wu=p["w_up_e"].astype(BF16),
        wd=p["w_down_e"].astype(BF16), fin=row(p["final_norm"]))


def _trunk(x, w):
    B, S, d = x.shape
    T = B * S
    x2d = x.reshape(T, d)
    tm = min(512, S)
    tabs = _rope_tables(S)
    q, k, v, xc, gelu_g, gatt, grec = _proj_call(
        x2d, B, S, tabs, w["mixg"], w["win"], w["qg"], w["wqb"], w["kvg"], w["wkvb"], w["gateb"],
        w["cw"], w["cb"], tm)
    o = _attn_call(q, k, v, tq=min(1024, S), tk=min(1024, S), unroll=4)

    L = min(256, S)
    xc3 = xc.reshape(B, S, d)
    gelu3 = gelu_g.reshape(B, S, d)
    hb = _lru_call(xc3, w["wlru"][1], w["ab"][1:2], w["xb"][1:2], w["lam"][1:2], None, None, L,
                   reverse=True)
    yl = _lru_call(xc3, w["wlru"][0], w["ab"][0:1], w["xb"][0:1], w["lam"][0:1], hb, gelu3, L,
                   reverse=False)

    x1, pk, meta, cnt = _merge_call(x2d, o.reshape(T, d), yl.reshape(T, d), gatt, grec, w["woa"],
                                    w["wol"], w["wout"], w["fg"], w["wr"], w["br"], tm=min(512, S))
    R = 256
    cnt, off, total, lo, hi, nv, blk, n_tiles = _routing_tables(cnt[0], T, R)
    is_b = meta[:, META_BUCKET][:, None] == jnp.arange(N_BUCKETS, dtype=jnp.int32)[None, :]
    slot = jnp.sum(jnp.where(is_b, off[None, :], 0), axis=1) + meta[:, META_RANK]
    xs = _dispatch_call(slot, off, cnt, total, pk, meta, n_tiles, R, tb=min(1024, T))
    ys = _expert_call(lo, hi, nv, blk, xs, w["wg"], w["wu"], w["wd"], R)
    y = _return_call(slot, ys, x1, w["fin"], tb=min(1024, T))
    return y.reshape(B, S, d)


def kernel(x_prompt, x_sample, mix_norm, w_in, q_a_norm, w_q_b, kv_a_norm, w_kv_b, w_o_attn, conv_w, conv_b, lru_a_w, lru_a_b, lru_x_w, lru_x_b, lru_lambda, w_o_lru, gate_b, w_out, ffn_norm, router_group_w, router_group_b, router_expert_w, router_expert_b, w_gate_e, w_up_e, w_down_e, final_norm):
    depth = mix_norm.shape[0]
    layers = []
    for l in range(depth):
        layers.append(_prep_params(dict(
            mix_norm=mix_norm[l], w_in=w_in[l], q_a_norm=q_a_norm[l], w_q_b=w_q_b[l],
            kv_a_norm=kv_a_norm[l], w_kv_b=w_kv_b[l], w_o_attn=w_o_attn[l], conv_w=conv_w[l],
            conv_b=conv_b[l], lru_a_w=lru_a_w[l], lru_a_b=lru_a_b[l], lru_x_w=lru_x_w[l],
            lru_x_b=lru_x_b[l], lru_lambda=lru_lambda[l], w_o_lru=w_o_lru[l], gate_b=gate_b[l],
            w_out=w_out[l], ffn_norm=ffn_norm[l], router_group_w=router_group_w[l],
            router_group_b=router_group_b[l], router_expert_w=router_expert_w[l],
            router_expert_b=router_expert_b[l], w_gate_e=w_gate_e[l], w_up_e=w_up_e[l],
            w_down_e=w_down_e[l], final_norm=final_norm)))
    assert depth == 1

    def trunk(x):
        for w in layers:
            x = _trunk(x, w)
        return x

    return (trunk(x_prompt), trunk(x_sample))
```

```python
import functools
import math

import jax
import jax.numpy as jnp
from jax import lax
from jax.experimental import pallas as pl
from jax.experimental.pallas import tpu as pltpu

N_HEADS = 8
QK_NOPE = 128
QK_ROPE = 64
V_HEAD = 128
Q_LORA = 384
KV_LORA = 256
ROPE_THETA = 10000.0
LRU_BLOCKS = 8
LRU_C = 8.0
N_GROUPS = 4
EXPERTS_PER_GROUP = 8
N_EXPERTS = N_GROUPS * EXPERTS_PER_GROUP
EPS = 1e-6

LANES = 128
HEAD_PAD = 256
LAT_COLS = Q_LORA + KV_LORA + LANES
PAIR_SLOTS = EXPERTS_PER_GROUP * EXPERTS_PER_GROUP
N_BUCKETS = N_GROUPS * PAIR_SLOTS
ROW = 8
ROW_PK = 4
META_BUCKET, META_RANK, META_W_LO, META_W_HI = 0, 1, 2, 3
HI16 = -65536
DMA_GROUP = 16
SQRT_FLOOR = 1e-36
VMEM_LIMIT = 56 * 1024 * 1024

F32 = jnp.float32
BF16 = jnp.bfloat16


def _sigmoid(z):
    return 1.0 / (1.0 + jnp.exp2(z * (-math.log2(math.e))))


def _rms(x, g):
    var = jnp.mean(x * x, axis=-1, keepdims=True)
    return x * lax.rsqrt(var + EPS) * g


def _rope(pe, cos_t, nsin_lo, sin_hi):
    x2_to_lo = pltpu.roll(pe, 96, axis=1)
    x1_to_hi = pltpu.roll(pe, 32, axis=1)
    return pe * cos_t + x2_to_lo * nsin_lo + x1_to_hi * sin_hi


def _conv_time(x, prev, nxt, cw_ref, cb_ref):
    L, W = x.shape
    row8 = lax.broadcasted_iota(jnp.int32, (8, W), 0)

    def shifted(k, head, tail):
        r = pltpu.roll(x, k % L, axis=0)
        return jnp.concatenate([head(r[0:8]), r[8:L - 8], tail(r[L - 8:L])], axis=0)

    keep = lambda s: s
    xm1 = shifted(1, lambda s: jnp.where(row8 == 0, prev[7:8, :], s), keep)
    xm2 = shifted(2, lambda s: jnp.where(row8 == 0, prev[6:7, :],
                                         jnp.where(row8 == 1, prev[7:8, :], s)), keep)
    xp1 = shifted(-1, keep, lambda s: jnp.where(row8 == 7, nxt[0:1, :], s))
    return cb_ref[...] + (xm2 * cw_ref[0:1, :] + xm1 * cw_ref[1:2, :] + x * cw_ref[2:3, :]
                          + xp1 * cw_ref[3:4, :])


def _proj_kernel(x_ref, xp_ref, xn_ref, cos_ref, nsl_ref, sh_ref, mixg_ref, win_ref, qg_ref,
                 wqb_ref, kvg_ref, wkvb_ref, gateb_ref, cw_ref, cb_ref, q_ref, k_ref, v_ref,
                 xc_ref, gelu_ref, gatt_ref, grec_ref, *, scale, nt_seq):
    d = x_ref.shape[-1]
    xn = _rms(x_ref[...], mixg_ref[...]).astype(BF16)
    lat = jnp.dot(xn, win_ref[:, 0:LAT_COLS], preferred_element_type=F32)
    cos_t, nsl, sh = cos_ref[...], nsl_ref[...], sh_ref[...]

    qn = _rms(lat[:, 0:Q_LORA], qg_ref[...]).astype(BF16)
    qf = jnp.dot(qn, wqb_ref[...], preferred_element_type=F32)
    for h in range(N_HEADS):
        base = h * HEAD_PAD
        q_ref[0, h, :, 0:LANES] = (qf[:, base:base + LANES] * scale).astype(BF16)
        pe = _rope(qf[:, base + LANES:base + HEAD_PAD], cos_t, nsl, sh)
        q_ref[0, h, :, LANES:HEAD_PAD] = (pe * scale).astype(BF16)

    kvn = _rms(lat[:, Q_LORA:Q_LORA + KV_LORA], kvg_ref[...]).astype(BF16)
    kvf = jnp.dot(kvn, wkvb_ref[...], preferred_element_type=F32)
    kpe = _rope(lat[:, Q_LORA + KV_LORA:LAT_COLS], cos_t, nsl, sh).astype(BF16)
    for h in range(N_HEADS):
        k_ref[0, h, :, 0:LANES] = kvf[:, h * LANES:(h + 1) * LANES].astype(BF16)
        k_ref[0, h, :, LANES:HEAD_PAD] = kpe
        v_ref[0, h, :, :] = kvf[:, (N_HEADS + h) * LANES:(N_HEADS + h + 1) * LANES].astype(BF16)

    c0 = LAT_COLS
    ti = pl.program_id(0) % nt_seq
    halo = jnp.concatenate([xp_ref[...], xn_ref[...]], axis=0)
    halo = jnp.dot(_rms(halo, mixg_ref[...]).astype(BF16), win_ref[:, c0:c0 + d],
                   preferred_element_type=F32)
    prev = jnp.where(ti > 0, halo[0:8], 0.0)
    nxt = jnp.where(ti < nt_seq - 1, halo[8:16], 0.0)
    xl = jnp.dot(xn, win_ref[:, c0:c0 + d], preferred_element_type=F32)
    xc_ref[...] = _conv_time(xl, prev, nxt, cw_ref, cb_ref)
    g = jnp.dot(xn, win_ref[:, c0 + d:c0 + 2 * d], preferred_element_type=F32)
    gelu_ref[...] = jax.nn.gelu(g).astype(BF16)
    ga = jnp.dot(xn, win_ref[:, c0 + 2 * d:c0 + 3 * d], preferred_element_type=F32)
    gatt_ref[...] = _sigmoid(ga + gateb_ref[0:1, :]).astype(BF16)
    gb = jnp.dot(xn, win_ref[:, c0 + 3 * d:c0 + 4 * d], preferred_element_type=F32)
    grec_ref[...] = _sigmoid(gb + gateb_ref[1:2, :]).astype(BF16)


def _proj_call(x2d, B, S, tabs, mixg, win, qg, wqb, kvg, wkvb, gateb, cw, cb, tm):
    T, d = x2d.shape
    nt_seq = S // tm
    r8 = tm // 8
    halo_prev = pl.BlockSpec((8, d), lambda i: (jnp.maximum(i * r8 - 1, 0), 0))
    halo_next = pl.BlockSpec((8, d), lambda i: (jnp.minimum((i + 1) * r8, T // 8 - 1), 0))
    full = lambda a: pl.BlockSpec(a.shape, lambda i: (0,) * a.ndim, pipeline_mode=pl.Buffered(1))
    tab_spec = pl.BlockSpec((tm, LANES), lambda i: (i % nt_seq, 0))
    tok = lambda w: pl.BlockSpec((tm, w), lambda i: (i, 0))
    head_spec = lambda w: pl.BlockSpec((1, N_HEADS, tm, w), lambda i: (i // nt_seq, 0, i % nt_seq, 0))
    scale = math.log2(math.e) / math.sqrt(QK_NOPE + QK_ROPE)
    return pl.pallas_call(
        functools.partial(_proj_kernel, scale=scale, nt_seq=nt_seq),
        grid=(T // tm,),
        in_specs=[tok(d), halo_prev, halo_next, tab_spec, tab_spec, tab_spec, full(mixg), full(win),
                  full(qg), full(wqb), full(kvg), full(wkvb), full(gateb), full(cw), full(cb)],
        out_specs=[head_spec(HEAD_PAD), head_spec(HEAD_PAD), head_spec(V_HEAD),
                   tok(d), tok(d), tok(d), tok(d)],
        out_shape=[
            jax.ShapeDtypeStruct((B, N_HEADS, S, HEAD_PAD), BF16),
            jax.ShapeDtypeStruct((B, N_HEADS, S, HEAD_PAD), BF16),
            jax.ShapeDtypeStruct((B, N_HEADS, S, V_HEAD), BF16),
            jax.ShapeDtypeStruct((T, d), F32),
            jax.ShapeDtypeStruct((T, d), BF16),
            jax.ShapeDtypeStruct((T, d), BF16),
            jax.ShapeDtypeStruct((T, d), BF16),
        ],
        compiler_params=pltpu.CompilerParams(dimension_semantics=("arbitrary",),
                                             vmem_limit_bytes=VMEM_LIMIT),
        name="proj",
    )(x2d, x2d, x2d, *tabs, mixg, win, qg, wqb, kvg, wkvb, gateb, cw, cb)


def _attn_kernel(q_ref, k_ref, v_ref, o_ref, s_ref, m_ref, acc_ref, *, tk, unroll):
    q = q_ref[0, 0]
    nk = k_ref.shape[2] // tk
    n_rep = tk // LANES

    def scores(j):
        off = j * tk if isinstance(j, int) else pl.multiple_of(j * tk, tk)
        kc = k_ref[0, 0, pl.ds(off, tk), :]
        return lax.dot_general(q, kc, (((1,), (1,)), ((), ())), preferred_element_type=F32)

    ones_col = (lax.broadcasted_iota(jnp.int32, (tk, LANES), 1) == 0).astype(BF16)

    def consume(j, s):
        off = j * tk if isinstance(j, int) else pl.multiple_of(j * tk, tk)
        m_old = m_ref[...]
        m_new = jnp.maximum(m_old, jnp.max(s, axis=-1, keepdims=True))
        alpha = jnp.exp2(m_old - m_new)
        p = jnp.exp2(s - jnp.tile(m_new, (1, n_rep)))
        v1 = jnp.concatenate([v_ref[0, 0, pl.ds(off, tk), :], ones_col], axis=1)
        pv = jnp.dot(p.astype(BF16), v1, preferred_element_type=F32)
        acc_ref[...] = jnp.tile(alpha, (1, 2)) * acc_ref[...] + pv
        m_ref[...] = m_new

    m_ref[...] = jnp.full(m_ref.shape, -jnp.inf, F32)
    acc_ref[...] = jnp.zeros(acc_ref.shape, F32)
    s_ref[0] = scores(0)

    def step(j, cur):
        s = s_ref[cur]
        s_ref[1 - cur] = scores(j + 1)
        consume(j, s)

    if nk <= unroll:
        for j in range(nk - 1):
            step(j, j % 2)
    else:
        assert unroll % 2 == 0 and nk % unroll == 0
        def group(g, _):
            for u in range(unroll):
                step(unroll * g + u, u % 2)
            return 0
        lax.fori_loop(0, nk // unroll - 1, group, 0)
        for j in range(nk - unroll, nk - 1):
            step(j, j % 2)
    consume(nk - 1, s_ref[(nk - 1) % 2])
    l = jnp.sum(acc_ref[:, V_HEAD:2 * V_HEAD], axis=-1, keepdims=True)
    o_ref[0] = (acc_ref[:, 0:V_HEAD] / l).astype(BF16)


def _attn_call(q, k, v, tq, tk, unroll=8):
    B, H, S, _ = q.shape
    return pl.pallas_call(
        functools.partial(_attn_kernel, tk=tk, unroll=unroll),
        grid=(B, H, S // tq),
        in_specs=[
            pl.BlockSpec((1, 1, tq, HEAD_PAD), lambda b, h, i: (b, h, i, 0)),
            pl.BlockSpec((1, 1, S, HEAD_PAD), lambda b, h, i: (b, h, 0, 0)),
            pl.BlockSpec((1, 1, S, V_HEAD), lambda b, h, i: (b, h, 0, 0)),
        ],
        out_specs=pl.BlockSpec((1, tq, V_HEAD), lambda b, h, i: (b, i, h)),
        out_shape=jax.ShapeDtypeStruct((B, S, H * V_HEAD), BF16),
        scratch_shapes=[pltpu.VMEM((2, tq, tk), F32), pltpu.VMEM((tq, LANES), F32),
                        pltpu.VMEM((tq, 2 * V_HEAD), F32)],
        compiler_params=pltpu.CompilerParams(
            dimension_semantics=("arbitrary", "arbitrary", "arbitrary"),
            vmem_limit_bytes=VMEM_LIMIT),
        name="attn",
    )(q, k, v)


def _lru_kernel(*refs, reverse):
    if reverse:
        (x_ref, w_ref, ab_ref, xb_ref, lam_ref, out_ref, h_ref, a_s, b_s, h_s) = refs
    else:
        (x_ref, w_ref, ab_ref, xb_ref, lam_ref, hb_ref, gelu_ref,
         out_ref, h_ref, a_s, b_s, h_s) = refs
    L, W = x_ref.shape[1], x_ref.shape[2]
    blk = W // LRU_BLOCKS

    @pl.when(pl.program_id(1) == 0)
    def _():
        h_ref[...] = jnp.zeros(h_ref.shape, F32)

    xc = x_ref[0]
    xcb = xc.astype(BF16)
    sp = jnp.log1p(jnp.exp(-jnp.abs(lam_ref[...]))) + jnp.maximum(-lam_ref[...], 0.0)
    sp2 = sp * (-LRU_C * math.log2(math.e))
    for n in range(LRU_BLOCKS):
        sl = slice(n * blk, (n + 1) * blk)
        z = jnp.dot(xcb[:, sl], w_ref[n], preferred_element_type=F32)
        r = _sigmoid(z[:, 0:blk] + ab_ref[:, sl])
        i = _sigmoid(z[:, blk:2 * blk] + xb_ref[:, sl])
        a = jnp.exp2(r * sp2[:, sl])
        a_s[:, sl] = a
        om = 1.0 - a * a
        root = om * lax.rsqrt(jnp.maximum(om, SQRT_FLOOR))
        b_s[:, sl] = root * (i * xc[:, sl])

    def step(t, h):
        tt = (L - 1 - t) if reverse else t
        h = a_s[pl.ds(tt, 1), :] * h + b_s[pl.ds(tt, 1), :]
        h_s[pl.ds(tt, 1), :] = h
        return h

    h_ref[...] = lax.fori_loop(0, L, step, h_ref[...], unroll=8)
    if reverse:
        out_ref[0] = h_s[...].astype(BF16)
    else:
        hsum = h_s[...] + hb_ref[0].astype(F32)
        out_ref[0] = (hsum * gelu_ref[0].astype(F32)).astype(BF16)


def _lru_call(xc, w, ab, xb, lam, hb, gelu, L, reverse):
    B, S, W = xc.shape
    nC = S // L
    pos = (lambda j: nC - 1 - j) if reverse else (lambda j: j)
    full = lambda a: pl.BlockSpec(a.shape, lambda b, j: (0,) * a.ndim)
    chunk = pl.BlockSpec((1, L, W), lambda b, j: (b, pos(j), 0))
    in_specs = [chunk, full(w), full(ab), full(xb), full(lam)]
    args = [xc, w, ab, xb, lam]
    if not reverse:
        in_specs += [chunk, chunk]
        args += [hb, gelu]
    return pl.pallas_call(
        functools.partial(_lru_kernel, reverse=reverse),
        grid=(B, nC),
        in_specs=in_specs,
        out_specs=chunk,
        out_shape=jax.ShapeDtypeStruct((B, S, W), BF16),
        scratch_shapes=[pltpu.VMEM((1, W), F32), pltpu.VMEM((L, W), F32), pltpu.VMEM((L, W), F32),
                        pltpu.VMEM((L, W), F32)],
        compiler_params=pltpu.CompilerParams(dimension_semantics=("arbitrary", "arbitrary"),
                                             vmem_limit_bytes=VMEM_LIMIT),
        name="lru_bwd" if reverse else "lru_fwd",
    )(*args)


def _merge_kernel(x_ref, o_ref, yl_ref, gatt_ref, grec_ref, woa_ref, wol_ref, wout_ref, fg_ref,
                  wr_ref, br_ref, x1_ref, pk_ref, meta_ref, cnt_ref):
    tm = x_ref.shape[0]
    y_attn = jnp.dot(o_ref[...], woa_ref[...], preferred_element_type=F32)
    y_lru = jnp.dot(yl_ref[...], wol_ref[...], preferred_element_type=F32)
    mix = gatt_ref[...].astype(F32) * y_attn + grec_ref[...].astype(F32) * y_lru
    x1 = x_ref[...] + jnp.dot(mix.astype(BF16), wout_ref[...], preferred_element_type=F32)
    xn2 = _rms(x1, fg_ref[...])

    x_hi = xn2.astype(BF16)
    x_lo = (xn2 - x_hi.astype(F32)).astype(BF16)
    hh = jnp.dot(x_hi, wr_ref[...], preferred_element_type=F32)
    lh = jnp.dot(x_lo, wr_ref[:, 0:LANES], preferred_element_type=F32)
    logits = hh[:, 0:LANES] + (hh[:, LANES:2 * LANES] + lh) + br_ref[...]
    lane = lax.broadcasted_iota(jnp.int32, logits.shape, 1)
    big = jnp.int32(LANES)
    neg = jnp.float32(-jnp.inf)
    is_g = (lane >= N_EXPERTS) & (lane < N_EXPERTS + N_GROUPS)
    gl = jnp.where(is_g, logits, neg)
    gmax = jnp.max(gl, axis=-1, keepdims=True)
    g_p = 1.0 / jnp.sum(jnp.exp(gl - gmax), axis=-1, keepdims=True)
    g_idx = jnp.min(jnp.where(gl == gmax, lane - N_EXPERTS, big), axis=-1, keepdims=True)
    in_grp = (lane < N_EXPERTS) & ((lane // EXPERTS_PER_GROUP) == g_idx)
    el = jnp.where(in_grp, logits, neg)
    m1 = jnp.max(el, axis=-1, keepdims=True)
    i1 = jnp.min(jnp.where(el == m1, lane, big), axis=-1, keepdims=True)
    el2 = jnp.where(lane == i1, neg, el)
    m2 = jnp.max(el2, axis=-1, keepdims=True)
    i2 = jnp.min(jnp.where(el2 == m2, lane, big), axis=-1, keepdims=True)
    e2 = jnp.exp(m2 - m1)
    w1 = g_p / (1.0 + e2)
    w2 = w1 * e2

    first_lo = i1 < i2
    e_lo = jnp.where(first_lo, i1, i2)
    e_hi = jnp.where(first_lo, i2, i1)
    w_lo = jnp.where(first_lo, w1, w2)
    w_hi = jnp.where(first_lo, w2, w1)
    bucket = g_idx * PAIR_SLOTS + (e_lo % EXPERTS_PER_GROUP) * EXPERTS_PER_GROUP + e_hi % EXPERTS_PER_GROUP

    @pl.when(pl.program_id(0) == 0)
    def _():
        cnt_ref[...] = jnp.zeros(cnt_ref.shape, F32)

    blane = lax.broadcasted_iota(jnp.int32, (tm, N_BUCKETS), 1)
    onehot = blane == bucket
    tri = (lax.broadcasted_iota(jnp.int32, (tm, tm), 0)
           > lax.broadcasted_iota(jnp.int32, (tm, tm), 1))
    earlier = jnp.dot(tri.astype(BF16), onehot.astype(BF16), preferred_element_type=F32)
    base = cnt_ref[0:1, :]
    rank = jnp.sum(jnp.where(onehot, earlier + base, 0.0), axis=-1, keepdims=True)
    cnt_ref[0:1, :] = base + jnp.sum(onehot.astype(F32), axis=0, keepdims=True)
    bits = lambda v: lax.bitcast_convert_type(v, jnp.int32)
    meta_ref[...] = jnp.where(
        lane == META_BUCKET, bucket,
        jnp.where(lane == META_RANK, rank.astype(jnp.int32),
                  jnp.where(lane == META_W_LO, bits(w_lo),
                            jnp.where(lane == META_W_HI, bits(w_hi), 0))))

    x1_ref[...] = x1
    half = x1.shape[1] // 2
    xb = x_hi.astype(F32)
    pk_ref[...] = (bits(xb[:, half:]) & HI16) | lax.shift_right_logical(bits(xb[:, :half]), 16)


def _merge_call(x2d, o, yl, gatt, grec, woa, wol, wout, fg, wr, br, tm):
    T, d = x2d.shape
    assert d == 2 * ROW_PK * LANES
    full = lambda a: pl.BlockSpec(a.shape, lambda i: (0,) * a.ndim)
    tok = lambda w: pl.BlockSpec((tm, w), lambda i: (i, 0))
    return pl.pallas_call(
        _merge_kernel,
        grid=(T // tm,),
        in_specs=[tok(d), tok(d), tok(d), tok(d), tok(d), full(woa), full(wol), full(wout),
                  full(fg), full(wr), full(br)],
        out_specs=[tok(d), tok(d // 2), tok(LANES),
                   pl.BlockSpec((8, N_BUCKETS), lambda i: (0, 0))],
        out_shape=[jax.ShapeDtypeStruct((T, d), F32),
                   jax.ShapeDtypeStruct((T, d // 2), jnp.int32),
                   jax.ShapeDtypeStruct((T, LANES), jnp.int32),
                   jax.ShapeDtypeStruct((8, N_BUCKETS), F32)],
        compiler_params=pltpu.CompilerParams(dimension_semantics=("arbitrary",),
                                             vmem_limit_bytes=VMEM_LIMIT),
        name="merge",
    )(x2d, o, yl, gatt, grec, woa, wol, wout, fg, wr, br)


def _row_copy(src, src_row, dst, dst_row, rows, sem):
    return pltpu.make_async_copy(src.at[pl.ds(src_row * rows, rows)],
                                 dst.at[pl.ds(dst_row * rows, rows)], sem)


def _start_rows(n_rows, slot_ref, copy_of):
    def start(g, _):
        rows = [g * DMA_GROUP + u for u in range(DMA_GROUP)]
        slots = [slot_ref[t] for t in rows]
        for t, s in zip(rows, slots):
            copy_of(t, s).start()
        return 0

    lax.fori_loop(0, n_rows // DMA_GROUP, start, 0)


def _wait_rows(n_rows, copy_of):
    def wait(g, _):
        for u in range(DMA_GROUP):
            copy_of(g * DMA_GROUP + u, 0).wait()
        return 0

    lax.fori_loop(0, n_rows // DMA_GROUP, wait, 0)


def _fill_unused_slots(cnt_ref, off_ref, total_ref, zero_ref, xs_ref, sem, R, n_tiles):
    zero_ref[...] = jnp.zeros(zero_ref.shape, zero_ref.dtype)
    bits = R.bit_length() - 1

    def run_copy(pos, size):
        return pltpu.make_async_copy(zero_ref.at[pl.ds(0, size * ROW)],
                                     xs_ref.at[pl.ds(pos * ROW, size * ROW)], sem)

    def bucket_pads(b, op):
        pad = (-cnt_ref[b]) & (R - 1)
        pos = off_ref[b] + cnt_ref[b]
        for k in range(bits):
            has = (pad >> k) & 1

            @pl.when(has == 1)
            def _():
                op(run_copy(pos, 1 << k))

            pos = pos + has * (1 << k)

    def tail_tile(i, op):
        op(run_copy(i * R, R))

    for op in (lambda c: c.start(), lambda c: c.wait()):
        lax.fori_loop(0, N_BUCKETS, lambda b, _: (bucket_pads(b, op), 0)[1], 0)
        lax.fori_loop(total_ref[0], n_tiles, lambda i, _: (tail_tile(i, op), 0)[1], 0)


def _dispatch_kernel(slot_ref, off_ref, cnt_ref, total_ref, pk_ref, meta_ref, xs_ref,
                     stage_ref, zero_ref, sem, fill_sem, *, tb, R, n_tiles):
    @pl.when(pl.program_id(0) == 0)
    def _():
        stage_ref[...] = jnp.zeros(stage_ref.shape, stage_ref.dtype)
        _fill_unused_slots(cnt_ref, off_ref, total_ref, zero_ref, xs_ref, fill_sem, R, n_tiles)

    for c in range(ROW_PK):
        stage_ref[pl.ds(c, tb, stride=ROW), :] = pk_ref[:, c * LANES:(c + 1) * LANES]
    stage_ref[pl.ds(ROW_PK, tb, stride=ROW), :] = meta_ref[...]
    copy = lambda t, slot: _row_copy(stage_ref, t, xs_ref, slot, ROW, sem)
    _start_rows(tb, slot_ref, copy)
    _wait_rows(tb, copy)


def _dispatch_call(slot, off, cnt, total, pk, meta, n_tiles, R, tb):
    T = slot.shape[0]
    assert pk.shape[1] == ROW_PK * LANES
    smem_blk = pl.BlockSpec((tb,), lambda i: (i,), memory_space=pltpu.SMEM)
    smem = pl.BlockSpec(memory_space=pltpu.SMEM)
    return pl.pallas_call(
        functools.partial(_dispatch_kernel, tb=tb, R=R, n_tiles=n_tiles),
        grid=(T // tb,),
        in_specs=[smem_blk, smem, smem, smem,
                  pl.BlockSpec((tb, ROW_PK * LANES), lambda i: (i, 0)),
                  pl.BlockSpec((tb, LANES), lambda i: (i, 0))],
        out_specs=pl.BlockSpec(memory_space=pl.ANY),
        out_shape=jax.ShapeDtypeStruct((n_tiles * R * ROW, LANES), jnp.int32),
        scratch_shapes=[pltpu.VMEM((tb * ROW, LANES), jnp.int32),
                        pltpu.VMEM((R * ROW, LANES), jnp.int32),
                        pltpu.SemaphoreType.DMA(()), pltpu.SemaphoreType.DMA(())],
        compiler_params=pltpu.CompilerParams(dimension_semantics=("arbitrary",),
                                             vmem_limit_bytes=VMEM_LIMIT),
        name="dispatch",
    )(slot, off, cnt, total, pk, meta)


def _expert_kernel(lo_ref, hi_ref, nv_ref, blk_ref, xs_ref, wgl_ref, wul_ref, wdl_ref,
                   wgh_ref, wuh_ref, wdh_ref, ys_ref, *, R):
    nv = nv_ref[pl.program_id(0)]

    @pl.when(nv == 0)
    def _():
        ys_ref[...] = jnp.zeros(ys_ref.shape, F32)

    @pl.when(nv > 0)
    def _():
        f32 = lambda v: lax.bitcast_convert_type(v, F32)
        words = [xs_ref[pl.ds(c, R, stride=ROW), :] for c in range(ROW_PK)]
        xn = jnp.concatenate([f32(w << 16) for w in words] + [f32(w & HI16) for w in words],
                             axis=1).astype(BF16)
        meta = f32(xs_ref[pl.ds(ROW_PK, R, stride=ROW), :])
        lane = lax.broadcasted_iota(jnp.int32, meta.shape, 1)
        w_lo = jnp.sum(jnp.where(lane == META_W_LO, meta, 0.0), axis=-1, keepdims=True)
        w_hi = jnp.sum(jnp.where(lane == META_W_HI, meta, 0.0), axis=-1, keepdims=True)

        def hidden(wg_ref, wu_ref, w):
            gt = jnp.dot(xn, wg_ref[0], preferred_element_type=F32)
            up = jnp.dot(xn, wu_ref[0], preferred_element_type=F32)
            return ((gt * _sigmoid(gt)) * up * w).astype(BF16)

        moe = (jnp.dot(hidden(wgl_ref, wul_ref, w_lo), wdl_ref[0], preferred_element_type=F32)
               + jnp.dot(hidden(wgh_ref, wuh_ref, w_hi), wdh_ref[0], preferred_element_type=F32))
        for c in range(ROW):
            ys_ref[pl.ds(c, R, stride=ROW), :] = moe[:, c * LANES:(c + 1) * LANES]


def _expert_call(tile_lo, tile_hi, tile_nv, tile_blk, xs, wg, wu, wd, R):
    n_tiles = tile_lo.shape[0]
    E, d, de = wg.shape
    assert d == ROW * LANES
    lo_map = lambda i, lo, hi, nv, blk: (lo[i], 0, 0)
    hi_map = lambda i, lo, hi, nv, blk: (hi[i], 0, 0)
    grid_spec = pltpu.PrefetchScalarGridSpec(
        num_scalar_prefetch=4,
        grid=(n_tiles,),
        in_specs=[pl.BlockSpec((R * ROW, LANES), lambda i, lo, hi, nv, blk: (blk[i], 0)),
                  pl.BlockSpec((1, d, de), lo_map), pl.BlockSpec((1, d, de), lo_map),
                  pl.BlockSpec((1, de, d), lo_map),
                  pl.BlockSpec((1, d, de), hi_map), pl.BlockSpec((1, d, de), hi_map),
                  pl.BlockSpec((1, de, d), hi_map)],
        out_specs=pl.BlockSpec((R * ROW, LANES), lambda i, lo, hi, nv, blk: (i, 0)),
    )
    return pl.pallas_call(
        functools.partial(_expert_kernel, R=R),
        grid_spec=grid_spec,
        out_shape=jax.ShapeDtypeStruct((n_tiles * R * ROW, LANES), F32),
        compiler_params=pltpu.CompilerParams(dimension_semantics=("arbitrary",),
                                             vmem_limit_bytes=VMEM_LIMIT),
        name="experts",
    )(tile_lo, tile_hi, tile_nv, tile_blk, xs, wg, wu, wd, wg, wu, wd)


def _return_kernel(slot_ref, slot_next_ref, ys_ref, x1_ref, fin_ref, y_ref, buf_ref, sem, *, tb):
    i = pl.program_id(0)
    cur = i % 2

    def copy_into(b):
        return lambda t, slot: _row_copy(ys_ref, slot, buf_ref.at[b], t, ROW, sem.at[b])

    @pl.when(i == 0)
    def _():
        _start_rows(tb, slot_ref, copy_into(0))

    @pl.when(i + 1 < pl.num_programs(0))
    def _():
        _start_rows(tb, slot_next_ref, copy_into(1 - cur))

    _wait_rows(tb, copy_into(cur))
    moe = jnp.concatenate([buf_ref[cur, pl.ds(c, tb, stride=ROW), :] for c in range(ROW)], axis=1)
    y_ref[...] = _rms(x1_ref[...] + moe, fin_ref[...])


def _return_call(slot, ys, x1, fin, tb):
    T, d = x1.shape
    n = T // tb
    return pl.pallas_call(
        functools.partial(_return_kernel, tb=tb),
        grid=(n,),
        in_specs=[pl.BlockSpec((tb,), lambda i: (i,), memory_space=pltpu.SMEM),
                  pl.BlockSpec((tb,), lambda i: (jnp.minimum(i + 1, n - 1),),
                               memory_space=pltpu.SMEM),
                  pl.BlockSpec(memory_space=pl.ANY),
                  pl.BlockSpec((tb, d), lambda i: (i, 0)),
                  pl.BlockSpec(fin.shape, lambda i: (0, 0))],
        out_specs=pl.BlockSpec((tb, d), lambda i: (i, 0)),
        out_shape=jax.ShapeDtypeStruct((T, d), F32),
        scratch_shapes=[pltpu.VMEM((2, tb * ROW, LANES), F32), pltpu.SemaphoreType.DMA((2,))],
        compiler_params=pltpu.CompilerParams(dimension_semantics=("arbitrary",),
                                             vmem_limit_bytes=VMEM_LIMIT),
        name="moe_return",
    )(slot, slot, ys, x1, fin)


def _routing_tables(counts, T, R):
    n_tiles = T // R + N_GROUPS * (EXPERTS_PER_GROUP * (EXPERTS_PER_GROUP - 1) // 2)
    cnt = counts.astype(jnp.int32)
    tiles_b = (cnt + R - 1) // R
    tile_end = jnp.cumsum(tiles_b)
    tile_start = tile_end - tiles_b
    off = tile_start * R
    total = tile_end[-1]
    i = jnp.arange(n_tiles, dtype=jnp.int32)
    blk = jnp.minimum(i, total - 1)
    b = jnp.sum((tile_end[None, :] <= blk[:, None]).astype(jnp.int32), axis=1)
    is_b = b[:, None] == jnp.arange(N_BUCKETS, dtype=jnp.int32)[None, :]
    look = lambda tab: jnp.sum(jnp.where(is_b, tab[None, :], 0), axis=1)
    nv = jnp.where(i < total, jnp.clip(look(cnt) - (blk - look(tile_start)) * R, 0, R), 0)
    grp = (b // PAIR_SLOTS) * EXPERTS_PER_GROUP
    lo = grp + (b % PAIR_SLOTS) // EXPERTS_PER_GROUP
    hi = grp + b % EXPERTS_PER_GROUP
    return cnt, off, total.reshape(1), lo, hi, nv.astype(jnp.int32), blk, n_tiles


def _rope_tables(S):
    inv = ROPE_THETA ** (-jnp.arange(0, QK_ROPE, 2, dtype=F32) / QK_ROPE)
    ang = jnp.arange(S, dtype=F32)[:, None] * inv[None, :]
    cos, sin = jnp.cos(ang), jnp.sin(ang)
    half = QK_ROPE // 2
    z = lambda n: jnp.zeros((S, n), F32)
    cos_t = jnp.concatenate([cos, cos, z(LANES - QK_ROPE)], axis=1)
    nsin_lo = jnp.concatenate([-sin, z(LANES - half)], axis=1)
    sin_hi = jnp.concatenate([z(half), sin, z(LANES - QK_ROPE)], axis=1)
    return cos_t, nsin_lo, sin_hi


def _prep_params(p):
    d = p["w_in"].shape[0]
    row = lambda a: a.reshape(1, -1).astype(F32)
    w_in = p["w_in"]
    c_q, c_kv, c_r = Q_LORA, Q_LORA + KV_LORA, Q_LORA + KV_LORA + QK_ROPE
    win = jnp.concatenate(
        [w_in[:, :c_r], jnp.zeros((d, LANES - QK_ROPE), F32), w_in[:, c_r:]], axis=1).astype(BF16)
    wqb = p["w_q_b"].reshape(Q_LORA, N_HEADS, QK_NOPE + QK_ROPE)
    wqb = jnp.pad(wqb, ((0, 0), (0, 0), (0, HEAD_PAD - QK_NOPE - QK_ROPE)))
    wqb = wqb.reshape(Q_LORA, N_HEADS * HEAD_PAD).astype(BF16)
    wkvb = p["w_kv_b"].reshape(KV_LORA, N_HEADS, QK_NOPE + V_HEAD)
    wkvb = jnp.concatenate([wkvb[:, :, :QK_NOPE].reshape(KV_LORA, -1),
                            wkvb[:, :, QK_NOPE:].reshape(KV_LORA, -1)], axis=1).astype(BF16)
    wlru = jnp.concatenate([p["lru_a_w"], p["lru_x_w"]], axis=-1).astype(BF16)
    wr = jnp.concatenate([p["router_expert_w"], p["router_group_w"],
                          jnp.zeros((d, LANES - N_EXPERTS - N_GROUPS), F32)], axis=1)
    wr_hi = wr.astype(BF16)
    wr = jnp.concatenate([wr_hi, (wr - wr_hi.astype(F32)).astype(BF16)], axis=1)
    br =jnp.concatenate([p["router_expert_b"], p["router_group_b"],
                          jnp.zeros((LANES - N_EXPERTS - N_GROUPS,), F32)]).reshape(1, LANES)
    return dict(
        mixg=row(p["mix_norm"]), win=win, qg=row(p["q_a_norm"]), wqb=wqb, kvg=row(p["kv_a_norm"]),
        wkvb=wkvb, gateb=p["gate_b"].astype(F32), cw=p["conv_w"].astype(F32), cb=row(p["conv_b"]),
        wlru=wlru, ab=p["lru_a_b"].astype(F32), xb=p["lru_x_b"].astype(F32),
        lam=p["lru_lambda"].astype(F32), woa=p["w_o_attn"].astype(BF16),
        wol=p["w_o_lru"].astype(BF16), wout=p["w_out"].astype(BF16), fg=row(p["ffn_norm"]),
        wr=wr, br=br, wg=p["w_gate_e"].astype(BF16), wu=p["w_up_e"].astype(BF16),
        wd=p["w_down_e"].astype(BF16), fin=row(p["final_norm"]))


def _trunk(x, w):
    B, S, d = x.shape
    T = B * S
    x2d = x.reshape(T, d)
    tm = min(512, S)
    tabs = _rope_tables(S)
    q, k, v, xc, gelu_g, gatt, grec = _proj_call(
        x2d, B, S, tabs, w["mixg"], w["win"], w["qg"], w["wqb"], w["kvg"], w["wkvb"], w["gateb"],
        w["cw"], w["cb"], tm)
    o = _attn_call(q, k, v, tq=min(1024, S), tk=min(1024, S), unroll=4)

    L = min(512, S)
    xc3 = xc.reshape(B, S, d)
    gelu3 = gelu_g.reshape(B, S, d)
    hb = _lru_call(xc3, w["wlru"][1], w["ab"][1:2], w["xb"][1:2], w["lam"][1:2], None, None, L,
                   reverse=True)
    yl = _lru_call(xc3, w["wlru"][0], w["ab"][0:1], w["xb"][0:1], w["lam"][0:1], hb, gelu3, L,
                   reverse=False)

    x1, pk, meta, cnt = _merge_call(x2d, o.reshape(T, d), yl.reshape(T, d), gatt, grec, w["woa"],
                                    w["wol"], w["wout"], w["fg"], w["wr"], w["br"], tm=min(512, S))
    R = 256
    cnt, off, total, lo, hi, nv, blk, n_tiles = _routing_tables(cnt[0], T, R)
    is_b = meta[:, META_BUCKET][:, None] == jnp.arange(N_BUCKETS, dtype=jnp.int32)[None, :]
    slot = jnp.sum(jnp.where(is_b, off[None, :], 0), axis=1) + meta[:, META_RANK]
    xs = _dispatch_call(slot, off, cnt, total, pk, meta, n_tiles, R, tb=min(2048, T))
    ys = _expert_call(lo, hi, nv, blk, xs, w["wg"], w["wu"], w["wd"], R)
    y = _return_call(slot, ys, x1, w["fin"], tb=min(1024, T))
    return y.reshape(B, S, d)


def kernel(x_prompt, x_sample, mix_norm, w_in, q_a_norm, w_q_b, kv_a_norm, w_kv_b, w_o_attn, conv_w, conv_b, lru_a_w, lru_a_b, lru_x_w, lru_x_b, lru_lambda, w_o_lru, gate_b, w_out, ffn_norm, router_group_w, router_group_b, router_expert_w, router_expert_b, w_gate_e, w_up_e, w_down_e, final_norm):
    depth = mix_norm.shape[0]
    layers = []
    for l in range(depth):
        layers.append(_prep_params(dict(
            mix_norm=mix_norm[l], w_in=w_in[l], q_a_norm=q_a_norm[l], w_q_b=w_q_b[l],
            kv_a_norm=kv_a_norm[l], w_kv_b=w_kv_b[l], w_o_attn=w_o_attn[l], conv_w=conv_w[l],
            conv_b=conv_b[l], lru_a_w=lru_a_w[l], lru_a_b=lru_a_b[l], lru_x_w=lru_x_w[l],
            lru_x_b=lru_x_b[l], lru_lambda=lru_lambda[l], w_o_lru=w_o_lru[l], gate_b=gate_b[l],
            w_out=w_out[l], ffn_norm=ffn_norm[l], router_group_w=router_group_w[l],
            router_group_b=router_group_b[l], router_expert_w=router_expert_w[l],
            router_expert_b=router_expert_b[l], w_gate_e=w_gate_e[l], w_up_e=w_up_e[l],
            w_down_e=w_down_e[l], final_norm=final_norm)))
    assert depth == 1

    def trunk(x):
        for w in layers:
            x = _trunk(x, w)
        return x

    return (trunk(x_prompt), trunk(x_sample))
```

```python
import functools
import math

import jax
import jax.numpy as jnp
from jax import lax
from jax.experimental import pallas as pl
from jax.experimental.pallas import tpu as pltpu

N_HEADS = 8
QK_NOPE = 128
QK_ROPE = 64
V_HEAD = 128
Q_LORA = 384
KV_LORA = 256
ROPE_THETA = 10000.0
LRU_BLOCKS = 8
LRU_C = 8.0
N_GROUPS = 4
EXPERTS_PER_GROUP = 8
N_EXPERTS = N_GROUPS * EXPERTS_PER_GROUP
EPS = 1e-6

LANES = 128
HEAD_PAD = 256
LAT_COLS = Q_LORA + KV_LORA + LANES
PAIR_SLOTS = EXPERTS_PER_GROUP * EXPERTS_PER_GROUP
N_BUCKETS = N_GROUPS * PAIR_SLOTS
ROW = 8
ROW_PK = 4
META_BUCKET, META_RANK, META_W_LO, META_W_HI = 0, 1, 2, 3
HI16 = -65536
DMA_GROUP = 16
SQRT_FLOOR = 1e-36
VMEM_LIMIT = 56 * 1024 * 1024

F32 = jnp.float32
BF16 = jnp.bfloat16


def _sigmoid(z):
    return 1.0 / (1.0 + jnp.exp2(z * (-math.log2(math.e))))


def _rms(x, g):
    var = jnp.mean(x * x, axis=-1, keepdims=True)
    return x * lax.rsqrt(var + EPS) * g


def _rope(pe, cos_t, nsin_lo, sin_hi):
    x2_to_lo = pltpu.roll(pe, 96, axis=1)
    x1_to_hi = pltpu.roll(pe, 32, axis=1)
    return pe * cos_t + x2_to_lo * nsin_lo + x1_to_hi * sin_hi


def _conv_time(x, prev, nxt, cw_ref, cb_ref):
    L, W = x.shape
    row8 = lax.broadcasted_iota(jnp.int32, (8, W), 0)

    def shifted(k, head, tail):
        r = pltpu.roll(x, k % L, axis=0)
        return jnp.concatenate([head(r[0:8]), r[8:L - 8], tail(r[L - 8:L])], axis=0)

    keep = lambda s: s
    xm1 = shifted(1, lambda s: jnp.where(row8 == 0, prev[7:8, :], s), keep)
    xm2 = shifted(2, lambda s: jnp.where(row8 == 0, prev[6:7, :],
                                         jnp.where(row8 == 1, prev[7:8, :], s)), keep)
    xp1 = shifted(-1, keep, lambda s: jnp.where(row8 == 7, nxt[0:1, :], s))
    return cb_ref[...] + (xm2 * cw_ref[0:1, :] + xm1 * cw_ref[1:2, :] + x * cw_ref[2:3, :]
                          + xp1 * cw_ref[3:4, :])


def _proj_kernel(x_ref, xp_ref, xn_ref, cos_ref, nsl_ref, sh_ref, mixg_ref, win_ref, qg_ref,
                 wqb_ref, kvg_ref, wkvb_ref, gateb_ref, cw_ref, cb_ref, q_ref, k_ref, v_ref,
                 xc_ref, gelu_ref, gatt_ref, grec_ref, *, scale, nt_seq):
    d = x_ref.shape[-1]
    xn = _rms(x_ref[...], mixg_ref[...]).astype(BF16)
    lat = jnp.dot(xn, win_ref[:, 0:LAT_COLS], preferred_element_type=F32)
    cos_t, nsl, sh = cos_ref[...], nsl_ref[...], sh_ref[...]

    qn = _rms(lat[:, 0:Q_LORA], qg_ref[...]).astype(BF16)
    qf = jnp.dot(qn, wqb_ref[...], preferred_element_type=F32)
    for h in range(N_HEADS):
        base = h * HEAD_PAD
        q_ref[0, h, :, 0:LANES] = (qf[:, base:base + LANES] * scale).astype(BF16)
        pe = _rope(qf[:, base + LANES:base + HEAD_PAD], cos_t, nsl, sh)
        q_ref[0, h, :, LANES:HEAD_PAD] = (pe * scale).astype(BF16)

    kvn = _rms(lat[:, Q_LORA:Q_LORA + KV_LORA], kvg_ref[...]).astype(BF16)
    kvf = jnp.dot(kvn, wkvb_ref[...], preferred_element_type=F32)
    kpe = _rope(lat[:, Q_LORA + KV_LORA:LAT_COLS], cos_t, nsl, sh).astype(BF16)
    for h in range(N_HEADS):
        k_ref[0, h, :, 0:LANES] = kvf[:, h * LANES:(h + 1) * LANES].astype(BF16)
        k_ref[0, h, :, LANES:HEAD_PAD] = kpe
        v_ref[0, h, :, :] = kvf[:, (N_HEADS + h) * LANES:(N_HEADS + h + 1) * LANES].astype(BF16)

    c0 = LAT_COLS
    ti = pl.program_id(0) % nt_seq
    halo = jnp.concatenate([xp_ref[...], xn_ref[...]], axis=0)
    halo = jnp.dot(_rms(halo, mixg_ref[...]).astype(BF16), win_ref[:, c0:c0 + d],
                   preferred_element_type=F32)
    prev = jnp.where(ti > 0, halo[0:8], 0.0)
    nxt = jnp.where(ti < nt_seq - 1, halo[8:16], 0.0)
    xl = jnp.dot(xn, win_ref[:, c0:c0 + d], preferred_element_type=F32)
    xc_ref[...] = _conv_time(xl, prev, nxt, cw_ref, cb_ref)
    g = jnp.dot(xn, win_ref[:, c0 + d:c0 + 2 * d], preferred_element_type=F32)
    gelu_ref[...] = jax.nn.gelu(g).astype(BF16)
    ga = jnp.dot(xn, win_ref[:, c0 + 2 * d:c0 + 3 * d], preferred_element_type=F32)
    gatt_ref[...] = _sigmoid(ga + gateb_ref[0:1, :]).astype(BF16)
    gb = jnp.dot(xn, win_ref[:, c0 + 3 * d:c0 + 4 * d], preferred_element_type=F32)
    grec_ref[...] = _sigmoid(gb + gateb_ref[1:2, :]).astype(BF16)


def _proj_call(x2d, B, S, tabs, mixg, win, qg, wqb, kvg, wkvb, gateb, cw, cb, tm):
    T, d = x2d.shape
    nt_seq = S // tm
    r8 = tm // 8
    halo_prev = pl.BlockSpec((8, d), lambda i: (jnp.maximum(i * r8 - 1, 0), 0))
    halo_next = pl.BlockSpec((8, d), lambda i: (jnp.minimum((i + 1) * r8, T // 8 - 1), 0))
    full = lambda a: pl.BlockSpec(a.shape, lambda i: (0,) * a.ndim, pipeline_mode=pl.Buffered(1))
    tab_spec = pl.BlockSpec((tm, LANES), lambda i: (i % nt_seq, 0))
    tok = lambda w: pl.BlockSpec((tm, w), lambda i: (i, 0))
    head_spec = lambda w: pl.BlockSpec((1, N_HEADS, tm, w), lambda i: (i // nt_seq, 0, i % nt_seq, 0))
    scale = math.log2(math.e) / math.sqrt(QK_NOPE + QK_ROPE)
    return pl.pallas_call(
        functools.partial(_proj_kernel, scale=scale, nt_seq=nt_seq),
        grid=(T // tm,),
        in_specs=[tok(d), halo_prev, halo_next, tab_spec, tab_spec, tab_spec, full(mixg), full(win),
                  full(qg), full(wqb), full(kvg), full(wkvb), full(gateb), full(cw), full(cb)],
        out_specs=[head_spec(HEAD_PAD), head_spec(HEAD_PAD), head_spec(V_HEAD),
                   tok(d), tok(d), tok(d), tok(d)],
        out_shape=[
            jax.ShapeDtypeStruct((B, N_HEADS, S, HEAD_PAD), BF16),
            jax.ShapeDtypeStruct((B, N_HEADS, S, HEAD_PAD), BF16),
            jax.ShapeDtypeStruct((B, N_HEADS, S, V_HEAD), BF16),
            jax.ShapeDtypeStruct((T, d), F32),
            jax.ShapeDtypeStruct((T, d), BF16),
            jax.ShapeDtypeStruct((T, d), BF16),
            jax.ShapeDtypeStruct((T, d), BF16),
        ],
        compiler_params=pltpu.CompilerParams(dimension_semantics=("arbitrary",),
                                             vmem_limit_bytes=VMEM_LIMIT),
        name="proj",
    )(x2d, x2d, x2d, *tabs, mixg, win, qg, wqb, kvg, wkvb, gateb, cw, cb)


def _attn_kernel(q_ref, k_ref, v_ref, o_ref, s_ref, m_ref, acc_ref, *, tk, unroll):
    q = q_ref[0, 0]
    nk = k_ref.shape[2] // tk
    n_rep = tk // LANES

    def scores(j):
        off = j * tk if isinstance(j, int) else pl.multiple_of(j * tk, tk)
        kc = k_ref[0, 0, pl.ds(off, tk), :]
        return lax.dot_general(q, kc, (((1,), (1,)), ((), ())), preferred_element_type=F32)

    ones_col = (lax.broadcasted_iota(jnp.int32, (tk, LANES), 1) == 0).astype(BF16)

    def consume(j, s):
        off = j * tk if isinstance(j, int) else pl.multiple_of(j * tk, tk)
        m_old = m_ref[...]
        m_new = jnp.maximum(m_old, jnp.max(s, axis=-1, keepdims=True))
        alpha = jnp.exp2(m_old - m_new)
        p = jnp.exp2(s - jnp.tile(m_new, (1, n_rep)))
        v1 = jnp.concatenate([v_ref[0, 0, pl.ds(off, tk), :], ones_col], axis=1)
        pv = jnp.dot(p.astype(BF16), v1, preferred_element_type=F32)
        acc_ref[...] = jnp.tile(alpha, (1, 2)) * acc_ref[...] + pv
        m_ref[...] = m_new

    m_ref[...] = jnp.full(m_ref.shape, -jnp.inf, F32)
    acc_ref[...] = jnp.zeros(acc_ref.shape, F32)
    s_ref[0] = scores(0)

    def step(j, cur):
        s = s_ref[cur]
        s_ref[1 - cur] = scores(j + 1)
        consume(j, s)

    if nk <= unroll:
        for j in range(nk - 1):
            step(j, j % 2)
    else:
        assert unroll % 2 == 0 and nk % unroll == 0
        def group(g, _):
            for u in range(unroll):
                step(unroll * g + u, u % 2)
            return 0
        lax.fori_loop(0, nk // unroll - 1, group, 0)
        for j in range(nk - unroll, nk - 1):
            step(j, j % 2)
    consume(nk - 1, s_ref[(nk - 1) % 2])
    l = jnp.sum(acc_ref[:, V_HEAD:2 * V_HEAD], axis=-1, keepdims=True)
    o_ref[0] = (acc_ref[:, 0:V_HEAD] / l).astype(BF16)


def _attn_call(q, k, v, tq, tk, unroll=8):
    B, H, S, _ = q.shape
    return pl.pallas_call(
        functools.partial(_attn_kernel, tk=tk, unroll=unroll),
        grid=(B, H, S // tq),
        in_specs=[
            pl.BlockSpec((1, 1, tq, HEAD_PAD), lambda b, h, i: (b, h, i, 0)),
            pl.BlockSpec((1, 1, S, HEAD_PAD), lambda b, h, i: (b, h, 0, 0)),
            pl.BlockSpec((1, 1, S, V_HEAD), lambda b, h, i: (b, h, 0, 0)),
        ],
        out_specs=pl.BlockSpec((1, tq, V_HEAD), lambda b, h, i: (b, i, h)),
        out_shape=jax.ShapeDtypeStruct((B, S, H * V_HEAD), BF16),
        scratch_shapes=[pltpu.VMEM((2, tq, tk), F32), pltpu.VMEM((tq, LANES), F32),
                        pltpu.VMEM((tq, 2 * V_HEAD), F32)],
        compiler_params=pltpu.CompilerParams(
            dimension_semantics=("arbitrary", "arbitrary", "arbitrary"),
            vmem_limit_bytes=VMEM_LIMIT),
        name="attn",
    )(q, k, v)


def _lru_kernel(*refs, reverse):
    if reverse:
        (x_ref, w_ref, ab_ref, xb_ref, lam_ref, out_ref, h_ref, a_s, b_s, h_s) = refs
    else:
        (x_ref, w_ref, ab_ref, xb_ref, lam_ref, hb_ref, gelu_ref,
         out_ref, h_ref, a_s, b_s, h_s) = refs
    L, W = x_ref.shape[1], x_ref.shape[2]
    blk = W // LRU_BLOCKS

    @pl.when(pl.program_id(1) == 0)
    def _():
        h_ref[...] = jnp.zeros(h_ref.shape, F32)

    xc = x_ref[0]
    xcb = xc.astype(BF16)
    sp = jnp.log1p(jnp.exp(-jnp.abs(lam_ref[...]))) + jnp.maximum(-lam_ref[...], 0.0)
    sp2 = sp * (-LRU_C * math.log2(math.e))
    for n in range(LRU_BLOCKS):
        sl = slice(n * blk, (n + 1) * blk)
        z = jnp.dot(xcb[:, sl], w_ref[n], preferred_element_type=F32)
        r = _sigmoid(z[:, 0:blk] + ab_ref[:, sl])
        i = _sigmoid(z[:, blk:2 * blk] + xb_ref[:, sl])
        a = jnp.exp2(r * sp2[:, sl])
        a_s[:, sl] = a
        om = 1.0 - a * a
        root = om * lax.rsqrt(jnp.maximum(om, SQRT_FLOOR))
        b_s[:, sl] = root * (i * xc[:, sl])

    def step(t, h):
        tt = (L - 1 - t) if reverse else t
        h = a_s[pl.ds(tt, 1), :] * h + b_s[pl.ds(tt, 1), :]
        h_s[pl.ds(tt, 1), :] = h
        return h

    h_ref[...] = lax.fori_loop(0, L, step, h_ref[...], unroll=8)
    if reverse:
        out_ref[0] = h_s[...].astype(BF16)
    else:
        hsum = h_s[...] + hb_ref[0].astype(F32)
        out_ref[0] = (hsum * gelu_ref[0].astype(F32)).astype(BF16)


def _lru_call(xc, w, ab, xb, lam, hb, gelu, L, reverse):
    B, S, W = xc.shape
    nC = S // L
    pos = (lambda j: nC - 1 - j) if reverse else (lambda j: j)
    full = lambda a: pl.BlockSpec(a.shape, lambda b, j: (0,) * a.ndim)
    chunk = pl.BlockSpec((1, L, W), lambda b, j: (b, pos(j), 0))
    in_specs = [chunk, full(w), full(ab), full(xb), full(lam)]
    args = [xc, w, ab, xb, lam]
    if not reverse:
        in_specs += [chunk, chunk]
        args += [hb, gelu]
    return pl.pallas_call(
        functools.partial(_lru_kernel, reverse=reverse),
        grid=(B, nC),
        in_specs=in_specs,
        out_specs=chunk,
        out_shape=jax.ShapeDtypeStruct((B, S, W), BF16),
        scratch_shapes=[pltpu.VMEM((1, W), F32), pltpu.VMEM((L, W), F32), pltpu.VMEM((L, W), F32),
                        pltpu.VMEM((L, W), F32)],
        compiler_params=pltpu.CompilerParams(dimension_semantics=("arbitrary", "arbitrary"),
                                             vmem_limit_bytes=VMEM_LIMIT),
        name="lru_bwd" if reverse else "lru_fwd",
    )(*args)


def _merge_kernel(x_ref, o_ref, yl_ref, gatt_ref, grec_ref, woa_ref, wol_ref, wout_ref, fg_ref,
                  wr_ref, br_ref, x1_ref, pk_ref, meta_ref, cnt_ref):
    tm = x_ref.shape[0]
    y_attn = jnp.dot(o_ref[...], woa_ref[...], preferred_element_type=F32)
    y_lru = jnp.dot(yl_ref[...], wol_ref[...], preferred_element_type=F32)
    mix = gatt_ref[...].astype(F32) * y_attn + grec_ref[...].astype(F32) * y_lru
    x1 = x_ref[...] + jnp.dot(mix.astype(BF16), wout_ref[...], preferred_element_type=F32)
    xn2 = _rms(x1, fg_ref[...])

    x_hi = xn2.astype(BF16)
    x_lo = (xn2 - x_hi.astype(F32)).astype(BF16)
    hh = jnp.dot(x_hi, wr_ref[...], preferred_element_type=F32)
    lh = jnp.dot(x_lo, wr_ref[:, 0:LANES], preferred_element_type=F32)
    logits = hh[:, 0:LANES] + (hh[:, LANES:2 * LANES] + lh) + br_ref[...]
    lane = lax.broadcasted_iota(jnp.int32, logits.shape, 1)
    big = jnp.int32(LANES)
    neg = jnp.float32(-jnp.inf)
    is_g = (lane >= N_EXPERTS) & (lane < N_EXPERTS + N_GROUPS)
    gl = jnp.where(is_g, logits, neg)
    gmax = jnp.max(gl, axis=-1, keepdims=True)
    g_p = 1.0 / jnp.sum(jnp.exp(gl - gmax), axis=-1, keepdims=True)
    g_idx = jnp.min(jnp.where(gl == gmax, lane - N_EXPERTS, big), axis=-1, keepdims=True)
    in_grp = (lane < N_EXPERTS) & ((lane // EXPERTS_PER_GROUP) == g_idx)
    el = jnp.where(in_grp, logits, neg)
    m1 = jnp.max(el, axis=-1, keepdims=True)
    i1 = jnp.min(jnp.where(el == m1, lane, big), axis=-1, keepdims=True)
    el2 = jnp.where(lane == i1, neg, el)
    m2 = jnp.max(el2, axis=-1, keepdims=True)
    i2 = jnp.min(jnp.where(el2 == m2, lane, big), axis=-1, keepdims=True)
    e2 = jnp.exp(m2 - m1)
    w1 = g_p / (1.0 + e2)
    w2 = w1 * e2

    first_lo = i1 < i2
    e_lo = jnp.where(first_lo, i1, i2)
    e_hi = jnp.where(first_lo, i2, i1)
    w_lo = jnp.where(first_lo, w1, w2)
    w_hi = jnp.where(first_lo, w2, w1)
    bucket = g_idx * PAIR_SLOTS + (e_lo % EXPERTS_PER_GROUP) * EXPERTS_PER_GROUP + e_hi % EXPERTS_PER_GROUP

    @pl.when(pl.program_id(0) == 0)
    def _():
        cnt_ref[...] = jnp.zeros(cnt_ref.shape, F32)

    blane = lax.broadcasted_iota(jnp.int32, (tm, N_BUCKETS), 1)
    onehot = blane == bucket
    tri = (lax.broadcasted_iota(jnp.int32, (tm, tm), 0)
           > lax.broadcasted_iota(jnp.int32, (tm, tm), 1))
    earlier = jnp.dot(tri.astype(BF16), onehot.astype(BF16), preferred_element_type=F32)
    base = cnt_ref[0:1, :]
    rank = jnp.sum(jnp.where(onehot, earlier + base, 0.0), axis=-1, keepdims=True)
    cnt_ref[0:1, :] = base + jnp.sum(onehot.astype(F32), axis=0, keepdims=True)
    bits = lambda v: lax.bitcast_convert_type(v, jnp.int32)
    meta_ref[...] = jnp.where(
        lane == META_BUCKET, bucket,
        jnp.where(lane == META_RANK, rank.astype(jnp.int32),
                  jnp.where(lane == META_W_LO, bits(w_lo),
                            jnp.where(lane == META_W_HI, bits(w_hi), 0))))

    x1_ref[...] = x1
    half = x1.shape[1] // 2
    xb = x_hi.astype(F32)
    pk_ref[...] = (bits(xb[:, half:]) & HI16) | lax.shift_right_logical(bits(xb[:, :half]), 16)


def _merge_call(x2d, o, yl, gatt, grec, woa, wol, wout, fg, wr, br, tm):
    T, d = x2d.shape
    assert d == 2 * ROW_PK * LANES
    full = lambda a: pl.BlockSpec(a.shape, lambda i: (0,) * a.ndim)
    tok = lambda w: pl.BlockSpec((tm, w), lambda i: (i, 0))
    return pl.pallas_call(
        _merge_kernel,
        grid=(T // tm,),
        in_specs=[tok(d), tok(d), tok(d), tok(d), tok(d), full(woa), full(wol), full(wout),
                  full(fg), full(wr), full(br)],
        out_specs=[tok(d), tok(d // 2), tok(LANES),
                   pl.BlockSpec((8, N_BUCKETS), lambda i: (0, 0))],
        out_shape=[jax.ShapeDtypeStruct((T, d), F32),
                   jax.ShapeDtypeStruct((T, d // 2), jnp.int32),
                   jax.ShapeDtypeStruct((T, LANES), jnp.int32),
                   jax.ShapeDtypeStruct((8, N_BUCKETS), F32)],
        compiler_params=pltpu.CompilerParams(dimension_semantics=("arbitrary",),
                                             vmem_limit_bytes=VMEM_LIMIT),
        name="merge",
    )(x2d, o, yl, gatt, grec, woa, wol, wout, fg, wr, br)


def _row_copy(src, src_row, dst, dst_row, rows, sem):
    return pltpu.make_async_copy(src.at[pl.ds(src_row * rows, rows)],
                                 dst.at[pl.ds(dst_row * rows, rows)], sem)


def _start_rows(n_rows, slot_ref, copy_of):
    def start(g, _):
        rows = [g * DMA_GROUP + u for u in range(DMA_GROUP)]
        slots = [slot_ref[t] for t in rows]
        for t, s in zip(rows, slots):
            copy_of(t, s).start()
        return 0

    lax.fori_loop(0, n_rows // DMA_GROUP, start, 0)


def _wait_rows(n_rows, copy_of):
    def wait(g, _):
        for u in range(DMA_GROUP):
            copy_of(g * DMA_GROUP + u, 0).wait()
        return 0

    lax.fori_loop(0, n_rows // DMA_GROUP, wait, 0)


def _fill_unused_slots(cnt_ref, off_ref, total_ref, zero_ref, xs_ref, sem, R, n_tiles):
    zero_ref[...] = jnp.zeros(zero_ref.shape, zero_ref.dtype)
    bits = R.bit_length() - 1

    def run_copy(pos, size):
        return pltpu.make_async_copy(zero_ref.at[pl.ds(0, size * ROW)],
                                     xs_ref.at[pl.ds(pos * ROW, size * ROW)], sem)

    def bucket_pads(b, op):
        pad = (-cnt_ref[b]) & (R - 1)
        pos = off_ref[b] + cnt_ref[b]
        for k in range(bits):
            has = (pad >> k) & 1

            @pl.when(has == 1)
            def _():
                op(run_copy(pos, 1 << k))

            pos = pos + has * (1 << k)

    def tail_tile(i, op):
        op(run_copy(i * R, R))

    for op in (lambda c: c.start(), lambda c: c.wait()):
        lax.fori_loop(0, N_BUCKETS, lambda b, _: (bucket_pads(b, op), 0)[1], 0)
        lax.fori_loop(total_ref[0], n_tiles, lambda i, _: (tail_tile(i, op), 0)[1], 0)


def _dispatch_kernel(slot_ref, off_ref, cnt_ref, total_ref, pk_ref, meta_ref, xs_ref,
                     stage_ref, zero_ref, sem, fill_sem, *, tb, R, n_tiles):
    i = pl.program_id(0)
    cur = i % 2

    @pl.when(i == 0)
    def _():
        stage_ref[...] = jnp.zeros(stage_ref.shape, stage_ref.dtype)
        _fill_unused_slots(cnt_ref, off_ref, total_ref, zero_ref, xs_ref, fill_sem, R, n_tiles)

    for c in range(ROW_PK):
        stage_ref[cur, pl.ds(c, tb, stride=ROW), :] = pk_ref[:, c * LANES:(c + 1) * LANES]
    stage_ref[cur, pl.ds(ROW_PK, tb, stride=ROW), :] = meta_ref[...]

    def copy_from(b):
        return lambda t, slot: _row_copy(stage_ref.at[b], t, xs_ref, slot, ROW, sem.at[b])

    _start_rows(tb, slot_ref, copy_from(cur))

    @pl.when(i > 0)
    def _():
        _wait_rows(tb, copy_from(1 - cur))

    @pl.when(i == pl.num_programs(0) - 1)
    def _():
        _wait_rows(tb, copy_from(cur))


def _dispatch_call(slot, off, cnt, total, pk, meta, n_tiles, R, tb):
    T = slot.shape[0]
    assert pk.shape[1] == ROW_PK * LANES
    smem_blk = pl.BlockSpec((tb,), lambda i: (i,), memory_space=pltpu.SMEM)
    smem = pl.BlockSpec(memory_space=pltpu.SMEM)
    return pl.pallas_call(
        functools.partial(_dispatch_kernel, tb=tb, R=R, n_tiles=n_tiles),
        grid=(T // tb,),
        in_specs=[smem_blk, smem, smem, smem,
                  pl.BlockSpec((tb, ROW_PK * LANES), lambda i: (i, 0)),
                  pl.BlockSpec((tb, LANES), lambda i: (i, 0))],
        out_specs=pl.BlockSpec(memory_space=pl.ANY),
        out_shape=jax.ShapeDtypeStruct((n_tiles * R * ROW, LANES), jnp.int32),
        scratch_shapes=[pltpu.VMEM((2, tb * ROW, LANES), jnp.int32),
                        pltpu.VMEM((R * ROW, LANES), jnp.int32),
                        pltpu.SemaphoreType.DMA((2,)), pltpu.SemaphoreType.DMA(())],
        compiler_params=pltpu.CompilerParams(dimension_semantics=("arbitrary",),
                                             vmem_limit_bytes=VMEM_LIMIT),
        name="dispatch",
    )(slot, off, cnt, total, pk, meta)


def _expert_kernel(lo_ref, hi_ref, nv_ref, blk_ref, xs_ref, wgl_ref, wul_ref, wdl_ref,
                   wgh_ref, wuh_ref, wdh_ref, ys_ref, *, R):
    nv = nv_ref[pl.program_id(0)]

    @pl.when(nv == 0)
    def _():
        ys_ref[...] = jnp.zeros(ys_ref.shape, F32)

    @pl.when(nv > 0)
    def _():
        f32 = lambda v: lax.bitcast_convert_type(v, F32)
        words = [xs_ref[pl.ds(c, R, stride=ROW), :] for c in range(ROW_PK)]
        xn = jnp.concatenate([f32(w << 16) for w in words] + [f32(w & HI16) for w in words],
                             axis=1).astype(BF16)
        meta = f32(xs_ref[pl.ds(ROW_PK, R, stride=ROW), :])
        lane = lax.broadcasted_iota(jnp.int32, meta.shape, 1)
        w_lo = jnp.sum(jnp.where(lane == META_W_LO, meta, 0.0), axis=-1, keepdims=True)
        w_hi = jnp.sum(jnp.where(lane == META_W_HI, meta, 0.0), axis=-1, keepdims=True)

        def hidden(wg_ref, wu_ref, w):
            gt = jnp.dot(xn, wg_ref[0], preferred_element_type=F32)
            up = jnp.dot(xn, wu_ref[0], preferred_element_type=F32)
            return ((gt * _sigmoid(gt)) * up * w).astype(BF16)

        moe = (jnp.dot(hidden(wgl_ref, wul_ref, w_lo), wdl_ref[0], preferred_element_type=F32)
               + jnp.dot(hidden(wgh_ref, wuh_ref, w_hi), wdh_ref[0], preferred_element_type=F32))
        for c in range(ROW):
            ys_ref[pl.ds(c, R, stride=ROW), :] = moe[:, c * LANES:(c + 1) * LANES]


def _expert_call(tile_lo, tile_hi, tile_nv, tile_blk, xs, wg, wu, wd, R):
    n_tiles = tile_lo.shape[0]
    E, d, de = wg.shape
    assert d == ROW * LANES
    lo_map = lambda i, lo, hi, nv, blk: (lo[i], 0, 0)
    hi_map = lambda i, lo, hi, nv, blk: (hi[i], 0, 0)
    grid_spec = pltpu.PrefetchScalarGridSpec(
        num_scalar_prefetch=4,
        grid=(n_tiles,),
        in_specs=[pl.BlockSpec((R * ROW, LANES), lambda i, lo, hi, nv, blk: (blk[i], 0)),
                  pl.BlockSpec((1, d, de), lo_map), pl.BlockSpec((1, d, de), lo_map),
                  pl.BlockSpec((1, de, d), lo_map),
                  pl.BlockSpec((1, d, de), hi_map), pl.BlockSpec((1, d, de), hi_map),
                  pl.BlockSpec((1, de, d), hi_map)],
        out_specs=pl.BlockSpec((R * ROW, LANES), lambda i, lo, hi, nv, blk: (i, 0)),
    )
    return pl.pallas_call(
        functools.partial(_expert_kernel, R=R),
        grid_spec=grid_spec,
        out_shape=jax.ShapeDtypeStruct((n_tiles * R * ROW, LANES), F32),
        compiler_params=pltpu.CompilerParams(dimension_semantics=("arbitrary",),
                                             vmem_limit_bytes=VMEM_LIMIT),
        name="experts",
    )(tile_lo, tile_hi, tile_nv, tile_blk, xs, wg, wu, wd, wg, wu, wd)


def _return_kernel(slot_ref, slot_next_ref, ys_ref, x1_ref, fin_ref, y_ref, buf_ref, sem, *, tb):
    i = pl.program_id(0)
    cur = i % 2

    def copy_into(b):
        return lambda t, slot: _row_copy(ys_ref, slot, buf_ref.at[b], t, ROW, sem.at[b])

    @pl.when(i == 0)
    def _():
        _start_rows(tb, slot_ref, copy_into(0))

    @pl.when(i + 1 < pl.num_programs(0))
    def _():
        _start_rows(tb, slot_next_ref, copy_into(1 - cur))

    _wait_rows(tb, copy_into(cur))
    moe = jnp.concatenate([buf_ref[cur, pl.ds(c, tb, stride=ROW), :] for c in range(ROW)], axis=1)
    y_ref[...] = _rms(x1_ref[...] + moe, fin_ref[...])


def _return_call(slot, ys, x1, fin, tb):
    T, d = x1.shape
    n = T // tb
    return pl.pallas_call(
        functools.partial(_return_kernel, tb=tb),
        grid=(n,),
        in_specs=[pl.BlockSpec((tb,), lambda i: (i,), memory_space=pltpu.SMEM),
                  pl.BlockSpec((tb,), lambda i: (jnp.minimum(i + 1, n - 1),),
                               memory_space=pltpu.SMEM),
                  pl.BlockSpec(memory_space=pl.ANY),
                  pl.BlockSpec((tb, d), lambda i: (i, 0)),
                  pl.BlockSpec(fin.shape, lambda i: (0, 0))],
        out_specs=pl.BlockSpec((tb, d), lambda i: (i, 0)),
        out_shape=jax.ShapeDtypeStruct((T, d), F32),
        scratch_shapes=[pltpu.VMEM((2, tb * ROW, LANES), F32), pltpu.SemaphoreType.DMA((2,))],
        compiler_params=pltpu.CompilerParams(dimension_semantics=("arbitrary",),
                                             vmem_limit_bytes=VMEM_LIMIT),
        name="moe_return",
    )(slot, slot, ys, x1, fin)


def _routing_tables(counts, T, R):
    n_tiles = T // R + N_GROUPS * (EXPERTS_PER_GROUP * (EXPERTS_PER_GROUP - 1) // 2)
    cnt = counts.astype(jnp.int32)
    tiles_b = (cnt + R - 1) // R
    tile_end = jnp.cumsum(tiles_b)
    tile_start = tile_end - tiles_b
    off = tile_start * R
    total = tile_end[-1]
    i = jnp.arange(n_tiles, dtype=jnp.int32)
    blk = jnp.minimum(i, total - 1)
    b = jnp.sum((tile_end[None, :] <= blk[:, None]).astype(jnp.int32), axis=1)
    is_b = b[:, None] == jnp.arange(N_BUCKETS, dtype=jnp.int32)[None, :]
    look = lambda tab: jnp.sum(jnp.where(is_b, tab[None, :], 0), axis=1)
    nv = jnp.where(i < total, jnp.clip(look(cnt) - (blk - look(tile_start)) * R, 0, R), 0)
    grp = (b // PAIR_SLOTS) * EXPERTS_PER_GROUP
    lo = grp + (b % PAIR_SLOTS) // EXPERTS_PER_GROUP
    hi = grp + b % EXPERTS_PER_GROUP
    return cnt, off, total.reshape(1), lo, hi, nv.astype(jnp.int32), blk, n_tiles


def _rope_tables(S):
    inv = ROPE_THETA ** (-jnp.arange(0, QK_ROPE, 2, dtype=F32) / QK_ROPE)
    ang = jnp.arange(S, dtype=F32)[:, None] * inv[None, :]
    cos, sin = jnp.cos(ang), jnp.sin(ang)
    half = QK_ROPE // 2
    z = lambda n: jnp.zeros((S, n), F32)
    cos_t = jnp.concatenate([cos, cos, z(LANES - QK_ROPE)], axis=1)
    nsin_lo = jnp.concatenate([-sin, z(LANES - half)], axis=1)
    sin_hi = jnp.concatenate([z(half), sin, z(LANES - QK_ROPE)], axis=1)
    return cos_t, nsin_lo, sin_hi


def _prep_params(p):
    d = p["w_in"].shape[0]
    row = lambda a: a.reshape(1, -1).astype(F32)
    w_in = p["w_in"]
    c_q, c_kv, c_r = Q_LORA, Q_LORA + KV_LORA, Q_LORA + KV_LORA + QK_ROPE
    win = jnp.concatenate(
        [w_in[:, :c_r], jnp.zeros((d, LANES - QK_ROPE), F32), w_in[:, c_r:]], axis=1).astype(BF16)
    wqb = p["w_q_b"].reshape(Q_LORA, N_HEADS, QK_NOPE + QK_ROPE)
    wqb = jnp.pad(wqb, ((0, 0), (0, 0), (0, HEAD_PAD - QK_NOPE - QK_ROPE)))
    wqb = wqb.reshape(Q_LORA, N_HEADS * HEAD_PAD).astype(BF16)
    wkvb = p["w_kv_b"].reshape(KV_LORA, N_HEADS, QK_NOPE + V_HEAD)
    wkvb = jnp.concatenate([wkvb[:, :, :QK_NOPE].reshape(KV_LORA, -1),
                            wkvb[:, :, QK_NOPE:].reshape(KV_LORA, -1)], axis=1).astype(BF16)
    wlru = jnp.concatenate([p["lru_a_w"], p["lru_x_w"]], axis=-1).astype(BF16)
    wr = jnp.concatenate([p["router_expert_w"], p["router_group_w"],
                          jnp.zeros((d, LANES - N_EXPERTS - N_GROUPS), F32)], axis=1)
    wr_hi = wr.astype(BF16)
    wr = jnp.concatenate([wr_hi, (wr - wr_hi.astype(F32)).astype(BF16)], axis=1)
    br =jnp.concatenate([p["router_expert_b"], p["router_group_b"],
                          jnp.zeros((LANES - N_EXPERTS - N_GROUPS,), F32)]).reshape(1, LANES)
    return dict(
        mixg=row(p["mix_norm"]), win=win, qg=row(p["q_a_norm"]), wqb=wqb, kvg=row(p["kv_a_norm"]),
        wkvb=wkvb, gateb=p["gate_b"].astype(F32), cw=p["conv_w"].astype(F32), cb=row(p["conv_b"]),
        wlru=wlru, ab=p["lru_a_b"].astype(F32), xb=p["lru_x_b"].astype(F32),
        lam=p["lru_lambda"].astype(F32), woa=p["w_o_attn"].astype(BF16),
        wol=p["w_o_lru"].astype(BF16), wout=p["w_out"].astype(BF16), fg=row(p["ffn_norm"]),
        wr=wr, br=br, wg=p["w_gate_e"].astype(BF16), wu=p["w_up_e"].astype(BF16),
        wd=p["w_down_e"].astype(BF16), fin=row(p["final_norm"]))


def _trunk(x, w):
    B, S, d = x.shape
    T = B * S
    x2d = x.reshape(T, d)
    tm = min(512, S)
    tabs = _rope_tables(S)
    q, k, v, xc, gelu_g, gatt, grec = _proj_call(
        x2d, B, S, tabs, w["mixg"], w["win"], w["qg"], w["wqb"], w["kvg"], w["wkvb"], w["gateb"],
        w["cw"], w["cb"], tm)
    o = _attn_call(q, k, v, tq=min(1024, S), tk=min(1024, S), unroll=4)

    L = min(512, S)
    xc3 = xc.reshape(B, S, d)
    gelu3 = gelu_g.reshape(B, S, d)
    hb = _lru_call(xc3, w["wlru"][1], w["ab"][1:2], w["xb"][1:2], w["lam"][1:2], None, None, L,
                   reverse=True)
    yl = _lru_call(xc3, w["wlru"][0], w["ab"][0:1], w["xb"][0:1], w["lam"][0:1], hb, gelu3, L,
                   reverse=False)

    x1, pk, meta, cnt = _merge_call(x2d, o.reshape(T, d), yl.reshape(T, d), gatt, grec, w["woa"],
                                    w["wol"], w["wout"], w["fg"], w["wr"], w["br"], tm=min(512, S))
    R = 256
    cnt, off, total, lo, hi, nv, blk, n_tiles = _routing_tables(cnt[0], T, R)
    is_b = meta[:, META_BUCKET][:, None] == jnp.arange(N_BUCKETS, dtype=jnp.int32)[None, :]
    slot = jnp.sum(jnp.where(is_b, off[None, :], 0), axis=1) + meta[:, META_RANK]
    xs = _dispatch_call(slot, off, cnt, total, pk, meta, n_tiles, R, tb=min(1024, T))
    ys = _expert_call(lo, hi, nv, blk, xs, w["wg"], w["wu"], w["wd"], R)
    y = _return_call(slot, ys, x1, w["fin"], tb=min(1024, T))
    return y.reshape(B, S, d)


def kernel(x_prompt, x_sample, mix_norm, w_in, q_a_norm, w_q_b, kv_a_norm, w_kv_b, w_o_attn, conv_w, conv_b, lru_a_w, lru_a_b, lru_x_w, lru_x_b, lru_lambda, w_o_lru, gate_b, w_out, ffn_norm, router_group_w, router_group_b, router_expert_w, router_expert_b, w_gate_e, w_up_e, w_down_e, final_norm):
    depth = mix_norm.shape[0]
    layers = []
    for l in range(depth):
        layers.append(_prep_params(dict(
            mix_norm=mix_norm[l], w_in=w_in[l], q_a_norm=q_a_norm[l], w_q_b=w_q_b[l],
            kv_a_norm=kv_a_norm[l], w_kv_b=w_kv_b[l], w_o_attn=w_o_attn[l], conv_w=conv_w[l],
            conv_b=conv_b[l], lru_a_w=lru_a_w[l], lru_a_b=lru_a_b[l], lru_x_w=lru_x_w[l],
            lru_x_b=lru_x_b[l], lru_lambda=lru_lambda[l], w_o_lru=w_o_lru[l], gate_b=gate_b[l],
            w_out=w_out[l], ffn_norm=ffn_norm[l], router_group_w=router_group_w[l],
            router_group_b=router_group_b[l], router_expert_w=router_expert_w[l],
            router_expert_b=router_expert_b[l], w_gate_e=w_gate_e[l], w_up_e=w_up_e[l],
            w_down_e=w_down_e[l], final_norm=final_norm)))
    assert depth == 1

    def trunk(x):
        for w in layers:
            x = _trunk(x, w)
        return x

    return (trunk(x_prompt), trunk(x_sample))
```
